```python
import math
import jax, jax.numpy as jnp
from jax import lax
import numpy as np

D_MODEL = 1024
BATCH = 8
SEQ = 4096
DEPTH = 2

GRID_W = 64
CTX_LEN = 256
EPS = 1e-6

D_LRU = D_MODEL // 4
LRU_HEADS = 4
LRU_HEAD_DIM = D_LRU // LRU_HEADS
CONV_W = 4
LRU_C = 8.0
MLA_V = 64
MLA_NOPE = 64
MLA_ROPE = 32
MLA_QK = MLA_NOPE + MLA_ROPE
D_MLA = D_MODEL // 2
MLA_HEADS = D_MLA // MLA_V
Q_RANK = 3 * D_MODEL // 8
KV_RANK = D_MODEL // 4
ROPE_FREQS = MLA_ROPE // 4
ROPE_BASE = 10000.0
D_POOL = D_MODEL // 4
POOL_WINDOWS = (2, 4, 8, 16)
POOL_GROUPS = len(POOL_WINDOWS)
POOL_GROUP_DIM = D_POOL // POOL_GROUPS
D_MIX = D_LRU + D_MLA + D_POOL
IN_SPLITS = (D_LRU, 2 * D_LRU, 2 * D_LRU + Q_RANK, 2 * D_LRU + Q_RANK + KV_RANK,
             2 * D_LRU + Q_RANK + KV_RANK + MLA_ROPE)
D_IN = IN_SPLITS[-1] + D_POOL
D_FF = 4 * D_MODEL
Q_BLOCK = 128

kernel_name = "hybrid_rglru_mla_pool_diffusion_block"

F32 = jnp.float32


def _rmsnorm(x, g):
    x32 = x.astype(F32)
    y = x32 * lax.rsqrt(jnp.mean(x32 * x32, axis=-1, keepdims=True) + EPS)
    return (y * g.astype(F32)).astype(x.dtype)


def _modulate(h, shift, scale):
    return h * (1.0 + scale) + shift


def _axial_rope_tables(L):
    rows = L // GRID_W
    row = jnp.repeat(jnp.arange(rows, dtype=F32), GRID_W)
    col = jnp.tile(jnp.arange(GRID_W, dtype=F32), rows)
    freqs = jnp.power(ROPE_BASE, -jnp.arange(ROPE_FREQS, dtype=F32) / ROPE_FREQS)
    ang = jnp.stack([row, col], axis=-1)[:, :, None] * freqs
    return jnp.cos(ang), jnp.sin(ang)


def _apply_axial_rope(x, cos, sin):
    B, L, H, _ = x.shape
    xs = x.astype(F32).reshape(B, L, H, 2, 2, ROPE_FREQS)
    x1, x2 = xs[..., 0, :], xs[..., 1, :]
    c = cos[None, :, None]
    s = sin[None, :, None]
    y = jnp.stack([x1 * c - x2 * s, x2 * c + x1 * s], axis=-2)
    return y.reshape(B, L, H, MLA_ROPE).astype(x.dtype)


def _dwconv(x, w, b):
    L = x.shape[1]
    left = (CONV_W - 1) // 2
    xp = jnp.pad(x, ((0, 0), (left, CONV_W - 1 - left), (0, 0)))
    y = sum(xp[:, k:k + L] * w[k] for k in range(CONV_W))
    return y + b


def _rglru_coeffs(xc, w_a, b_a, w_x, b_x, lam):
    B, L, _ = xc.shape
    xh = xc.reshape(B, L, LRU_HEADS, LRU_HEAD_DIM)
    r = jax.nn.sigmoid((jnp.einsum('blhi,hij->blhj', xh, w_a).reshape(B, L, D_LRU) + b_a).astype(F32))
    i = jax.nn.sigmoid((jnp.einsum('blhi,hij->blhj', xh, w_x).reshape(B, L, D_LRU) + b_x).astype(F32))
    log_a = -LRU_C * r * jax.nn.softplus(-lam.astype(F32))
    a = jnp.exp(log_a)
    b = jnp.sqrt(-jnp.expm1(2.0 * log_a)) * (i * xc.astype(F32))
    return a, b


def _linear_scan(a, b, h0):
    def combine(e1, e2):
        a1, b1 = e1
        a2, b2 = e2
        return a1 * a2, a2 * b1 + b2
    A, H = lax.associative_scan(combine, (a, b), axis=1)
    return H + A * h0[:, None, :]


def _bidir_rglru(xc_lat, xc_ctx, w_a, b_a, w_x, b_x, lam):
    B = xc_lat.shape[0]
    zeros = jnp.zeros((B, D_LRU), F32)
    y_lat = 0.0
    y_ctx = 0.0
    for d in range(2):
        xl = xc_lat if d == 0 else jnp.flip(xc_lat, axis=1)
        xcx = xc_ctx if d == 0 else jnp.flip(xc_ctx, axis=1)
        a_c, b_c = _rglru_coeffs(xcx, w_a[d], b_a[d], w_x[d], b_x[d], lam[d])
        a_l, b_l = _rglru_coeffs(xl, w_a[d], b_a[d], w_x[d], b_x[d], lam[d])
        h_c = _linear_scan(a_c, b_c, zeros)
        h_l = _linear_scan(a_l, b_l, h_c[:, -1])
        if d == 1:
            h_c = jnp.flip(h_c, axis=1)
            h_l = jnp.flip(h_l, axis=1)
        y_lat = y_lat + h_l
        y_ctx = y_ctx + h_c
    return y_lat.astype(xc_lat.dtype), y_ctx.astype(xc_ctx.dtype)


def _mla_q(q_lat, g_q_lat, w_uq, g_qn, rope):
    B, L, _ = q_lat.shape
    q = (_rmsnorm(q_lat, g_q_lat) @ w_uq).reshape(B, L, MLA_HEADS, MLA_QK)
    q = _rmsnorm(q, g_qn)
    if rope is not None:
        q = jnp.concatenate([q[..., :MLA_NOPE], _apply_axial_rope(q[..., MLA_NOPE:], *rope)], axis=-1)
    return q


def _mla_kv(kv_lat, k_rope, g_kv_lat, w_ukv, g_kn, rope):
    B, L, _ = kv_lat.shape
    kv = (_rmsnorm(kv_lat, g_kv_lat) @ w_ukv).reshape(B, L, MLA_HEADS, MLA_NOPE + MLA_V)
    k_nope, v = kv[..., :MLA_NOPE], kv[..., MLA_NOPE:]
    k_r = jnp.broadcast_to(k_rope[:, :, None, :], (B, L, MLA_HEADS, MLA_ROPE))
    k = _rmsnorm(jnp.concatenate([k_nope, k_r], axis=-1), g_kn)
    if rope is not None:
        k = jnp.concatenate([k[..., :MLA_NOPE], _apply_axial_rope(k[..., MLA_NOPE:], *rope)], axis=-1)
    return k, v


def _block_attention(q, k, v):
    B, L, H, Dq = q.shape
    nb = L // Q_BLOCK
    qb = jnp.moveaxis(q.reshape(B, nb, Q_BLOCK, H, Dq), 1, 0)
    scale = 1.0 / math.sqrt(Dq)

    def attend(qi):
        s = jnp.einsum('bqhd,bkhd->bhqk', qi, k).astype(F32) * scale
        p = jax.nn.softmax(s, axis=-1)
        return jnp.einsum('bhqk,bkhd->bqhd', p.astype(v.dtype), v)

    o = lax.map(attend, qb)
    return jnp.moveaxis(o, 0, 1).reshape(B, L, H * v.shape[-1])


def _pool_mixer(u, w_pool, pool_scale):
    B, L, _ = u.shape
    ug = u.astype(F32).reshape(B, L, POOL_GROUPS, POOL_GROUP_DIM)
    csum = jnp.concatenate([jnp.zeros((B, 1, POOL_GROUPS, POOL_GROUP_DIM), F32),
                            jnp.cumsum(ug, axis=1)], axis=1)
    t = jnp.arange(L)
    means = []
    for g, w in enumerate(POOL_WINDOWS):
        lo = jnp.clip(t - w // 2, 0, L)
        hi = jnp.clip(t - w // 2 + w, 0, L)
        cnt = (hi - lo).astype(F32)[None, :, None]
        means.append((csum[:, hi, g] - csum[:, lo, g]) / cnt)
    mixed = jnp.stack(means, axis=2) - ug
    y = jnp.einsum('blgi,gij->blgj', mixed, w_pool.astype(F32)).reshape(B, L, D_POOL)
    return (y * pool_scale.astype(F32)).astype(u.dtype)


def _mixer(h_lat, h_ctx, rope, need_ctx_out, w_in, conv_w, conv_b, lru_w_a, lru_b_a, lru_w_x, lru_b_x,
           lru_lambda, g_q_lat, w_uq, g_kv_lat, w_ukv, g_qn, g_kn, w_pool, pool_scale, w_out):
    z_lat = h_lat @ w_in
    z_ctx = h_ctx @ w_in
    lx_l, lg_l, ql_l, kvl_l, kr_l, pu_l = jnp.split(z_lat, IN_SPLITS, axis=-1)
    lx_c, lg_c, ql_c, kvl_c, kr_c, pu_c = jnp.split(z_ctx, IN_SPLITS, axis=-1)
    r_l, r_c = _bidir_rglru(_dwconv(lx_l, conv_w, conv_b), _dwconv(lx_c, conv_w, conv_b),
                            lru_w_a, lru_b_a, lru_w_x, lru_b_x, lru_lambda)
    a_l = r_l * jax.nn.gelu(lg_l)
    k_c, v_c = _mla_kv(kvl_c, kr_c, g_kv_lat, w_ukv, g_kn, None)
    k_l, v_l = _mla_kv(kvl_l, kr_l[:, :, :], g_kv_lat, w_ukv, g_kn, rope)
    q_l = _mla_q(ql_l, g_q_lat, w_uq, g_qn, rope)
    att_l = _block_attention(q_l, jnp.concatenate([k_c, k_l], axis=1), jnp.concatenate([v_c, v_l], axis=1))
    p_l = _pool_mixer(pu_l, w_pool, pool_scale)
    y_lat = jnp.concatenate([a_l, att_l, p_l], axis=-1) @ w_out
    if not need_ctx_out:
        return y_lat, None
    a_c = r_c * jax.nn.gelu(lg_c)
    q_c = _mla_q(ql_c, g_q_lat, w_uq, g_qn, None)
    att_c = _block_attention(q_c, k_c, v_c)
    p_c = _pool_mixer(pu_c, w_pool, pool_scale)
    y_ctx = jnp.concatenate([a_c, att_c, p_c], axis=-1) @ w_out
    return y_lat, y_ctx


def _mlp(h, w1, w2):
    return jnp.square(jax.nn.relu(h @ w1)) @ w2


def setup_inputs(seed: int = 0) -> dict:
    key = jax.random.key(seed)
    ks = jax.random.split(key, 32)
    nrm = lambda k, shape, s: jax.random.normal(k, shape, F32) * s
    a0 = jax.random.uniform(ks[12], (DEPTH, 2, D_LRU), F32, minval=0.9, maxval=0.999)
    s0 = a0 ** (1.0 / LRU_C)
    return {
        "x": nrm(ks[0], (BATCH, SEQ, D_MODEL), 1.0),
        "c": nrm(ks[1], (BATCH, D_MODEL), 1.0),
        "ctx": nrm(ks[2], (BATCH, CTX_LEN, D_MODEL), 1.0),
        "c_ctx": nrm(ks[3], (D_MODEL,), 1.0),
        "w_mod": nrm(ks[4], (DEPTH, D_MODEL, 6 * D_MODEL), 0.5 * D_MODEL ** -0.5),
        "b_mod": nrm(ks[5], (DEPTH, 6 * D_MODEL), 0.02),
        "g_norm1": 1.0 + nrm(ks[6], (DEPTH, D_MODEL), 0.02),
        "g_norm2": 1.0 + nrm(ks[7], (DEPTH, D_MODEL), 0.02),
        "w_in": nrm(ks[8], (DEPTH, D_MODEL, D_IN), D_MODEL ** -0.5),
        "conv_w": nrm(ks[9], (DEPTH, CONV_W, D_LRU), CONV_W ** -0.5),
        "conv_b": nrm(ks[10], (DEPTH, D_LRU), 0.02),
        "lru_w_a": nrm(ks[11], (DEPTH, 2, LRU_HEADS, LRU_HEAD_DIM, LRU_HEAD_DIM), LRU_HEAD_DIM ** -0.5),
        "lru_b_a": nrm(ks[13], (DEPTH, 2, D_LRU), 0.02),
        "lru_w_x": nrm(ks[14], (DEPTH, 2, LRU_HEADS, LRU_HEAD_DIM, LRU_HEAD_DIM), LRU_HEAD_DIM ** -0.5),
        "lru_b_x": nrm(ks[15], (DEPTH, 2, D_LRU), 0.02),
        "lru_lambda": jnp.log(s0) - jnp.log1p(-s0),
        "g_q_lat": 1.0 + nrm(ks[16], (DEPTH, Q_RANK), 0.02),
        "w_uq": nrm(ks[17], (DEPTH, Q_RANK, MLA_HEADS * MLA_QK), Q_RANK ** -0.5),
        "g_kv_lat": 1.0 + nrm(ks[18], (DEPTH, KV_RANK), 0.02),
        "w_ukv": nrm(ks[19], (DEPTH, KV_RANK, MLA_HEADS * (MLA_NOPE + MLA_V)), KV_RANK ** -0.5),
        "g_qn": 1.0 + nrm(ks[20], (DEPTH, MLA_QK), 0.02),
        "g_kn": 1.0 + nrm(ks[21], (DEPTH, MLA_QK), 0.02),
        "w_pool": nrm(ks[22], (DEPTH, POOL_GROUPS, POOL_GROUP_DIM, POOL_GROUP_DIM), POOL_GROUP_DIM ** -0.5),
        "pool_scale": 1.0 + nrm(ks[23], (DEPTH, D_POOL), 0.1),
        "w_out": nrm(ks[24], (DEPTH, D_MIX, D_MODEL), D_MIX ** -0.5),
        "w_ff1": nrm(ks[25], (DEPTH, D_MODEL, D_FF), D_MODEL ** -0.5),
        "w_ff2": nrm(ks[26], (DEPTH, D_FF, D_MODEL), D_FF ** -0.5),
    }


def reference(x, c, ctx, c_ctx, w_mod, b_mod, g_norm1, g_norm2, w_in, conv_w, conv_b, lru_w_a, lru_b_a,
              lru_w_x, lru_b_x, lru_lambda, g_q_lat, w_uq, g_kv_lat, w_ukv, g_qn, g_kn, w_pool, pool_scale,
              w_out, w_ff1, w_ff2):
    L = x.shape[1]
    rope = _axial_rope_tables(L)
    c_act = jax.nn.silu(c)
    cc_act = jax.nn.silu(c_ctx)
    h = ctx
    for l in range(DEPTH):
        last = l == DEPTH - 1
        mod_l = (c_act @ w_mod[l] + b_mod[l])[:, None, :]
        mod_c = (cc_act @ w_mod[l] + b_mod[l])[None, None, :]
        sh1, sc1, g1, sh2, sc2, g2 = jnp.split(mod_l, 6, axis=-1)
        csh1, csc1, cg1, csh2, csc2, cg2 = jnp.split(mod_c, 6, axis=-1)
        hx = _modulate(_rmsnorm(x, g_norm1[l]), sh1, sc1)
        hc = _modulate(_rmsnorm(h, g_norm1[l]), csh1, csc1)
        y_lat, y_ctx = _mixer(hx, hc, rope, not last, w_in[l], conv_w[l], conv_b[l], lru_w_a[l], lru_b_a[l],
                              lru_w_x[l], lru_b_x[l], lru_lambda[l], g_q_lat[l], w_uq[l], g_kv_lat[l],
                              w_ukv[l], g_qn[l], g_kn[l], w_pool[l], pool_scale[l], w_out[l])
        x = x + g1 * y_lat
        x = x + g2 * _mlp(_modulate(_rmsnorm(x, g_norm2[l]), sh2, sc2), w_ff1[l], w_ff2[l])
        if not last:
            h = h + cg1 * y_ctx
            h = h + cg2 * _mlp(_modulate(_rmsnorm(h, g_norm2[l]), csh2, csc2), w_ff1[l], w_ff2[l])
    return x
```

```python
import functools
import math

import jax
import jax.numpy as jnp
import numpy as np
from jax import lax
from jax.experimental import pallas as pl
from jax.experimental.pallas import tpu as pltpu

F32 = jnp.float32
BF16 = jnp.bfloat16

EPS = 1e-6
GRID_W = 64
LRU_HEADS = 4
LRU_C = 8.0
CONV_W = 4
MLA_V = 64
MLA_NOPE = 64
MLA_ROPE = 32
MLA_QK = MLA_NOPE + MLA_ROPE
ROPE_FREQS = MLA_ROPE // 4
ROPE_BASE = 10000.0
POOL_WINDOWS = (2, 4, 8, 16)

LANES = 128
SUBLANES = 8
HEAD_PAD = LANES
HALO = 16
VMEM_LIMIT = 56 * 1024 * 1024

LOG2E = 1.4426950408889634


def _cparams(n_grid):
    return pltpu.CompilerParams(dimension_semantics=("arbitrary",) * n_grid,
                                vmem_limit_bytes=VMEM_LIMIT)


def _const_spec(shape):
    zeros = (0,) * len(shape)
    return pl.BlockSpec(shape, lambda *_: zeros)


def _mod_kernel(c_ref, w_ref, b_ref, o_ref):
    c = c_ref[...]
    act = (c * jax.nn.sigmoid(c)).astype(BF16)
    o_ref[0] = jnp.dot(act, w_ref[0].astype(BF16), preferred_element_type=F32) + b_ref[0]


def _mod_call(cvec, w_mod, b_mod):
    depth, d, n = w_mod.shape
    rows = cvec.shape[0]
    tn = 1024
    return pl.pallas_call(
        _mod_kernel,
        grid=(depth, n // tn),
        in_specs=[pl.BlockSpec((rows, d), lambda l, j: (0, 0)),
                  pl.BlockSpec((1, d, tn), lambda l, j: (l, 0, j)),
                  pl.BlockSpec((1, 1, tn), lambda l, j: (l, 0, j))],
        out_specs=pl.BlockSpec((1, rows, tn), lambda l, j: (l, 0, j)),
        out_shape=jax.ShapeDtypeStruct((depth, rows, n), F32),
        compiler_params=_cparams(2),
        name="adaln_mod",
    )(cvec, w_mod, b_mod.reshape(depth, 1, n))


def _inproj_kernel(x_ref, sh_ref, sc_ref, g1_ref, win_ref, gql_ref, wuq_ref, gkvl_ref, wk_ref,
                   wv_ref, gq_ref, gk_ref, ct_ref, st_ref,
                   lx_ref, lg_ref, pu_ref, q_ref, k_ref, v_ref, *, d_lru, q_rank, kv_rank, heads):
    x = x_ref[0]
    xn = x * lax.rsqrt(jnp.mean(x * x, axis=-1, keepdims=True) + EPS)
    hx = (xn * g1_ref[...]) * (1.0 + sc_ref[0]) + sh_ref[0]
    z = jnp.dot(hx.astype(BF16), win_ref[...], preferred_element_type=F32)
    o = 0
    lx_ref[0] = z[:, o:o + d_lru]; o += d_lru
    lg_ref[0] = z[:, o:o + d_lru]; o += d_lru
    ql = z[:, o:o + q_rank]; o += q_rank
    kvl = z[:, o:o + kv_rank]; o += kv_rank
    krb = z[:, o:o + HEAD_PAD]; o += HEAD_PAD
    pu_ref[0] = z[:, o:]

    qln = ql * lax.rsqrt(jnp.mean(ql * ql, axis=-1, keepdims=True) + EPS) * gql_ref[...]
    qf = jnp.dot(qln.astype(BF16), wuq_ref[...], preferred_element_type=F32)
    kvn = kvl * lax.rsqrt(jnp.mean(kvl * kvl, axis=-1, keepdims=True) + EPS) * gkvl_ref[...]
    kvn = kvn.astype(BF16)
    kf = jnp.dot(kvn, wk_ref[...], preferred_element_type=F32)
    vf = jnp.dot(kvn, wv_ref[...], preferred_element_type=F32)

    lane = lax.broadcasted_iota(jnp.int32, (1, HEAD_PAD), 1)
    real = (lane < MLA_QK).astype(F32)
    ct = ct_ref[...]
    st = st_ref[...]
    gq = gq_ref[...]
    gk = gk_ref[...]
    inv_qk = 1.0 / MLA_QK
    for h in range(heads):
        qh = qf[:, h * HEAD_PAD:(h + 1) * HEAD_PAD]
        ss = jnp.sum(qh * qh * real, axis=-1, keepdims=True) * inv_qk
        qh = qh * lax.rsqrt(ss + EPS) * gq
        qh = qh * ct + pltpu.roll(qh, HEAD_PAD - MLA_ROPE, axis=1) * st
        q_ref[0, h] = qh.astype(BF16)
        kh = kf[:, h * HEAD_PAD:(h + 1) * HEAD_PAD] + krb
        ss = jnp.sum(kh * kh * real, axis=-1, keepdims=True) * inv_qk
        kh = kh * lax.rsqrt(ss + EPS) * gk
        kh = kh * ct + pltpu.roll(kh, HEAD_PAD - MLA_ROPE, axis=1) * st
        k_ref[0, h] = kh.astype(BF16)
    for p in range(heads // 2):
        v_ref[0, p] = vf[:, p * LANES:(p + 1) * LANES].astype(BF16)


def _inproj_call(x, sh, sc, g1, lw, ct, st, tm):
    B, T, D = x.shape
    heads = lw["heads"]
    d_lru, q_rank, kv_rank = lw["d_lru"], lw["q_rank"], lw["kv_rank"]
    d_pool = lw["w_in"].shape[1] - (2 * d_lru + q_rank + kv_rank + HEAD_PAD)
    tok = lambda w: pl.BlockSpec((1, tm, w), lambda b, i: (b, i, 0))
    vec = pl.BlockSpec((1, 1, D), lambda b, i: (b, 0, 0))
    hd = lambda n: pl.BlockSpec((1, n, tm, LANES), lambda b, i: (b, 0, i, 0))
    kern = functools.partial(_inproj_kernel, d_lru=d_lru, q_rank=q_rank, kv_rank=kv_rank, heads=heads)
    return pl.pallas_call(
        kern,
        grid=(B, T // tm),
        in_specs=[tok(D), vec, vec, _const_spec((1, D)), _const_spec(lw["w_in"].shape),
                  _const_spec((1, q_rank)), _const_spec(lw["w_uq"].shape),
                  _const_spec((1, kv_rank)), _const_spec(lw["w_k"].shape), _const_spec(lw["w_v"].shape),
                  _const_spec((1, HEAD_PAD)), _const_spec((1, HEAD_PAD)),
                  pl.BlockSpec((tm, HEAD_PAD), lambda b, i: (i, 0)),
                  pl.BlockSpec((tm, HEAD_PAD), lambda b, i: (i, 0))],
        out_specs=[tok(d_lru), tok(d_lru), tok(d_pool), hd(heads), hd(heads), hd(heads // 2)],
        out_shape=[jax.ShapeDtypeStruct((B, T, d_lru), F32),
                   jax.ShapeDtypeStruct((B, T, d_lru), F32),
                   jax.ShapeDtypeStruct((B, T, d_pool), F32),
                   jax.ShapeDtypeStruct((B, heads, T, LANES), BF16),
                   jax.ShapeDtypeStruct((B, heads, T, LANES), BF16),
                   jax.ShapeDtypeStruct((B, heads // 2, T, LANES), BF16)],
        compiler_params=_cparams(2),
        name="in_proj",
    )(x, sh, sc, g1, lw["w_in"], lw["g_q_lat"], lw["w_uq"], lw["g_kv_lat"], lw["w_k"], lw["w_v"],
      lw["gq"], lw["gk"], ct, st)


def _roll_rows(a, shift):
    return pltpu.roll(a, shift % a.shape[0], axis=0)


def _scan8(a, b, reverse):
    n = a.shape[0]
    a3 = a.reshape(n // SUBLANES, SUBLANES, a.shape[1])
    b3 = b.reshape(n // SUBLANES, SUBLANES, b.shape[1])
    rid = lax.broadcasted_iota(jnp.int32, a3.shape, 1)
    for s in (1, 2, 4):
        if reverse:
            m = rid < SUBLANES - s
            sh = SUBLANES - s
        else:
            m = rid >= s
            sh = s
        a_sh = pltpu.roll(a3, sh, axis=1)
        b_sh = pltpu.roll(b3, sh, axis=1)
        b3 = jnp.where(m, a3 * b_sh + b3, b3)
        a3 = jnp.where(m, a3 * a_sh, a3)
    return a3, b3


def _seq_kernel(lx_ref, lg_ref, pu_ref, h0_ref, cw_ref, cb_ref, wg_ref, bg_ref, lam_ref, wp_ref,
                ps_ref, a_out_ref, p_out_ref, hfin_ref, xpad_ref, upad_ref, hf_ref, *, tc):
    T = lx_ref.shape[1]
    C = lx_ref.shape[2]
    nc = T // tc
    zeros_halo = jnp.zeros((HALO, C), F32)
    xpad_ref[0:HALO, :] = zeros_halo
    xpad_ref[HALO + T:HALO + T + HALO, :] = zeros_halo
    upad_ref[0:HALO, :] = zeros_halo
    upad_ref[HALO + T:HALO + T + HALO, :] = zeros_halo

    def fill(c, carry):
        r0 = pl.multiple_of(c * tc, tc)
        xpad_ref[pl.ds(HALO + r0, tc), :] = lx_ref[0, pl.ds(r0, tc), :]
        upad_ref[pl.ds(HALO + r0, tc), :] = pu_ref[0, pl.ds(r0, tc), :]
        return carry

    lax.fori_loop(0, nc, fill, 0)

    lam = lam_ref[...]
    coef = -LRU_C * (jnp.maximum(-lam, 0.0) + jnp.log(1.0 + jnp.exp(-jnp.abs(lam))))
    cw = cw_ref[...]
    cb = cb_ref[...]
    bg = bg_ref[...]

    def conv_chunk(r0):
        w = xpad_ref[pl.ds(r0 + HALO - SUBLANES, tc + 2 * SUBLANES), :]
        y = (cw[0:1] * _roll_rows(w, 1) + cw[1:2] * w + cw[2:3] * _roll_rows(w, -1)
             + cw[3:4] * _roll_rows(w, -2))
        return y[SUBLANES:SUBLANES + tc] + cb

    def coeffs(xc, d):
        g = jnp.dot(xc.astype(BF16), wg_ref[:, 2 * C * d:2 * C * (d + 1)],
                    preferred_element_type=F32) + bg[:, 2 * C * d:2 * C * (d + 1)]
        r = jax.nn.sigmoid(g[:, :C])
        i = jax.nn.sigmoid(g[:, C:])
        log_a = coef[d:d + 1] * r
        a = jnp.exp(log_a)
        b = jnp.sqrt(1.0 - a * a) * (i * xc)
        return a, b

    lane = lax.broadcasted_iota(jnp.int32, (1, C), 1)
    gdim = C // len(POOL_WINDOWS)

    def pool_chunk(r0):
        w = upad_ref[pl.ds(r0, tc + 2 * HALO), :]
        s2 = w + _roll_rows(w, 1)
        s4 = _roll_rows(s2, 1) + _roll_rows(s2, -1)
        s8 = _roll_rows(s4, 2) + _roll_rows(s4, -2)
        s16 = _roll_rows(s8, 4) + _roll_rows(s8, -4)
        t = r0 + lax.broadcasted_iota(jnp.int32, (tc, 1), 0)
        sel = None
        cnt = None
        for g, (win, s) in enumerate(zip(POOL_WINDOWS, (s2, s4, s8, s16))):
            sg = s[HALO:HALO + tc]
            cg = (jnp.minimum(t - win // 2 + win, T) - jnp.maximum(t - win // 2, 0)).astype(F32)
            if sel is None:
                sel, cnt = sg, jnp.broadcast_to(cg, (tc, C))
            else:
                m = lane >= g * gdim
                sel = jnp.where(m, sg, sel)
                cnt = jnp.where(m, cg, cnt)
        mixed = sel / cnt - w[HALO:HALO + tc]
        y = jnp.dot(mixed.astype(BF16), wp_ref[...], preferred_element_type=F32) * ps_ref[...]
        p_out_ref[0, pl.ds(r0, tc), :] = y.astype(p_out_ref.dtype)

    nv = tc // SUBLANES

    def fwd(c, h):
        r0 = pl.multiple_of(c * tc, tc)
        pool_chunk(r0)
        a, b = coeffs(conv_chunk(r0), 0)
        a3, b3 = _scan8(a, b, reverse=False)
        for j in range(nv):
            hj = b3[j] + a3[j] * h
            hf_ref[pl.ds(r0 + j * SUBLANES, SUBLANES), :] = hj
            h = jnp.broadcast_to(hj[SUBLANES - 1:SUBLANES], (SUBLANES, C))
        return h

    h = lax.fori_loop(0, nc, fwd, jnp.broadcast_to(h0_ref[0, 0:1, :], (SUBLANES, C)))
    hfin_ref[0, 0:1, :] = h[0:1]

    def bwd(cc, h):
        c = nc - 1 - cc
        r0 = pl.multiple_of(c * tc, tc)
        a, b = coeffs(conv_chunk(r0), 1)
        a3, b3 = _scan8(a, b, reverse=True)
        for j in range(nv - 1, -1, -1):
            hj = b3[j] + a3[j] * h
            rows = pl.ds(r0 + j * SUBLANES, SUBLANES)
            y = hj + hf_ref[rows, :]
            a_out_ref[0, rows, :] = (y * jax.nn.gelu(lg_ref[0, rows, :])).astype(a_out_ref.dtype)
            h = jnp.broadcast_to(hj[0:1], (SUBLANES, C))
        return h

    h = lax.fori_loop(0, nc, bwd, jnp.broadcast_to(h0_ref[0, 1:2, :], (SUBLANES, C)))
    hfin_ref[0, 1:2, :] = h[0:1]


def _seq_call(lx, lg, pu, h0, lw, tc):
    B, T, C = lx.shape
    seq = pl.BlockSpec((1, T, C), lambda b: (b, 0, 0))
    st = pl.BlockSpec((1, 2, C), lambda b: (b, 0, 0))
    return pl.pallas_call(
        functools.partial(_seq_kernel, tc=tc),
        grid=(B,),
        in_specs=[seq, seq, seq, st, _const_spec((CONV_W, C)), _const_spec((1, C)),
                  _const_spec((C, 4 * C)), _const_spec((1, 4 * C)), _const_spec((2, C)),
                  _const_spec((C, C)), _const_spec((1, C))],
        out_specs=[seq, seq, st],
        out_shape=[jax.ShapeDtypeStruct((B, T, C), BF16), jax.ShapeDtypeStruct((B, T, C), BF16),
                   jax.ShapeDtypeStruct((B, 2, C), F32)],
        scratch_shapes=[pltpu.VMEM((T + 2 * HALO, C), F32), pltpu.VMEM((T + 2 * HALO, C), F32),
                        pltpu.VMEM((T, C), F32)],
        compiler_params=_cparams(1),
        name="seq_mix",
    )(lx, lg, pu, h0, lw["conv_w"], lw["conv_b"], lw["w_gate"], lw["b_gate"], lw["lam"],
      lw["w_pool"], lw["pool_scale"])


def _attn_kernel(*refs, n_seg):
    q_ref = refs[0]
    kv_refs = refs[1:1 + 2 * n_seg]
    o_ref = refs[1 + 2 * n_seg]
    lane = lax.broadcasted_iota(jnp.int32, (1, LANES), 1)
    outs = []
    for hh in range(2):
        q = q_ref[0, hh]
        ss = [lax.dot_general(q, kv_refs[2 * s][0, hh], (((1,), (1,)), ((), ())),
                              preferred_element_type=F32) for s in range(n_seg)]
        m = ss[0].max(axis=-1, keepdims=True)
        for s in ss[1:]:
            m = jnp.maximum(m, s.max(axis=-1, keepdims=True))
        l = None
        acc = None
        for s_idx, s in enumerate(ss):
            p = jnp.exp2(s - m)
            ls = jnp.sum(p, axis=-1, keepdims=True)
            pv = jnp.dot(p.astype(BF16), kv_refs[2 * s_idx + 1][0, 0], preferred_element_type=F32)
            l = ls if l is None else l + ls
            acc = pv if acc is None else acc + pv
        outs.append(acc / l)
    o_ref[0] = jnp.where(lane < MLA_V, outs[0], outs[1]).astype(o_ref.dtype)


def _attn_call(q, segs, tq):
    B, H, T, _ = q.shape
    in_specs = [pl.BlockSpec((1, 2, tq, LANES), lambda b, p, i: (b, p, i, 0))]
    args = [q]
    for k, v in segs:
        lk = k.shape[2]
        in_specs.append(pl.BlockSpec((1, 2, lk, LANES), lambda b, p, i: (b, p, 0, 0)))
        in_specs.append(pl.BlockSpec((1, 1, lk, LANES), lambda b, p, i: (b, p, 0, 0)))
        args += [k, v]
    return pl.pallas_call(
        functools.partial(_attn_kernel, n_seg=len(segs)),
        grid=(B, H // 2, T // tq),
        in_specs=in_specs,
        out_specs=pl.BlockSpec((1, tq, LANES), lambda b, p, i: (b, i, p)),
        out_shape=jax.ShapeDtypeStruct((B, T, H * MLA_V), BF16),
        compiler_params=_cparams(3),
        name="attention",
    )(*args)


def _post_kernel(x_ref, a_ref, att_ref, p_ref, g1_ref, sh_ref, sc_ref, g2_ref, gn_ref, wo_ref,
                 w1_ref, w2_ref, o_ref, *, ff_chunk):
    d_lru = a_ref.shape[2]
    d_att = att_ref.shape[2]
    y = jnp.dot(a_ref[0], wo_ref[0:d_lru, :], preferred_element_type=F32)
    y += jnp.dot(att_ref[0], wo_ref[d_lru:d_lru + d_att, :], preferred_element_type=F32)
    y += jnp.dot(p_ref[0], wo_ref[d_lru + d_att:, :], preferred_element_type=F32)
    x1 = x_ref[0] + g1_ref[0] * y
    xn = x1 * lax.rsqrt(jnp.mean(x1 * x1, axis=-1, keepdims=True) + EPS)
    h = ((xn * gn_ref[...]) * (1.0 + sc_ref[0]) + sh_ref[0]).astype(BF16)
    d_ff = w1_ref.shape[1]
    acc = None
    for c in range(d_ff // ff_chunk):
        u = jnp.dot(h, w1_ref[:, c * ff_chunk:(c + 1) * ff_chunk], preferred_element_type=F32)
        u = jnp.square(jnp.maximum(u, 0.0)).astype(BF16)
        part = jnp.dot(u, w2_ref[c * ff_chunk:(c + 1) * ff_chunk, :], preferred_element_type=F32)
        acc = part if acc is None else acc + part
    o_ref[0] = x1 + g2_ref[0] * acc


def _post_call(x, a, att, p, g1, sh, sc, g2, gn, lw, tm):
    B, T, D = x.shape
    tok = lambda w: pl.BlockSpec((1, tm, w), lambda b, i: (b, i, 0))
    vec = pl.BlockSpec((1, 1, D), lambda b, i: (b, 0, 0))
    single = lambda shape: pl.BlockSpec(shape, lambda b, i: (0,) * len(shape),
                                        pipeline_mode=pl.Buffered(1))
    return pl.pallas_call(
        functools.partial(_post_kernel, ff_chunk=1024),
        grid=(B, T // tm),
        in_specs=[tok(D), tok(a.shape[2]), tok(att.shape[2]), tok(p.shape[2]), vec, vec, vec, vec,
                  _const_spec((1, D)), single(lw["w_out"].shape), single(lw["w_ff1"].shape),
                  single(lw["w_ff2"].shape)],
        out_specs=tok(D),
        out_shape=jax.ShapeDtypeStruct((B, T, D), F32),
        compiler_params=_cparams(2),
        name="out_proj_mlp",
    )(x, a, att, p, g1, sh, sc, g2, gn, lw["w_out"], lw["w_ff1"], lw["w_ff2"])


def _rope_partner():
    j = np.arange(MLA_ROPE)
    half = (j % (2 * ROPE_FREQS)) // ROPE_FREQS
    return np.where(half == 0, j + ROPE_FREQS, j - ROPE_FREQS)


def _block_diag(w):
    n, a, b = w.shape
    out = jnp.zeros((n * a, n * b), w.dtype)
    for i in range(n):
        out = out.at[i * a:(i + 1) * a, i * b:(i + 1) * b].set(w[i])
    return out


def _layer_weights(l, w_in, conv_w, conv_b, lru_w_a, lru_b_a, lru_w_x, lru_b_x, lru_lambda, g_q_lat,
                   w_uq, g_kv_lat, w_ukv, g_qn, g_kn, w_pool, pool_scale, w_out, w_ff1, w_ff2):
    d = w_in.shape[1]
    d_lru = conv_w.shape[2]
    q_rank = w_uq.shape[1]
    kv_rank = w_ukv.shape[1]
    heads = w_uq.shape[2] // MLA_QK
    partner = _rope_partner()
    o_kr = 2 * d_lru + q_rank + kv_rank
    wi = w_in[l]
    kr = wi[:, o_kr:o_kr + MLA_ROPE]
    krb = jnp.concatenate([jnp.zeros((d, MLA_NOPE), F32), kr, kr[:, partner]], axis=1)
    w_in_p = jnp.concatenate([wi[:, :o_kr], krb, wi[:, o_kr + MLA_ROPE:]], axis=1).astype(BF16)
    wq = w_uq[l].reshape(q_rank, heads, MLA_QK)
    wq = jnp.concatenate([wq, wq[:, :, MLA_NOPE + partner]], axis=2).reshape(q_rank, heads * HEAD_PAD)
    wkv = w_ukv[l].reshape(kv_rank, heads, MLA_NOPE + MLA_V)
    wk = jnp.concatenate([wkv[:, :, :MLA_NOPE], jnp.zeros((kv_rank, heads, HEAD_PAD - MLA_NOPE), F32)],
                         axis=2).reshape(kv_rank, heads * HEAD_PAD)
    wv = wkv[:, :, MLA_NOPE:].reshape(kv_rank, heads * MLA_V)
    gq = jnp.concatenate([g_qn[l], g_qn[l][MLA_NOPE + partner]]) * (LOG2E / math.sqrt(MLA_QK))
    gk = jnp.concatenate([g_kn[l], g_kn[l][MLA_NOPE + partner]])
    w_gate = jnp.concatenate([_block_diag(lru_w_a[l, 0]), _block_diag(lru_w_x[l, 0]),
                              _block_diag(lru_w_a[l, 1]), _block_diag(lru_w_x[l, 1])], axis=1)
    b_gate = jnp.concatenate([lru_b_a[l, 0], lru_b_x[l, 0], lru_b_a[l, 1], lru_b_x[l, 1]])
    return dict(
        heads=heads, d_lru=d_lru, q_rank=q_rank, kv_rank=kv_rank,
        w_in=w_in_p, g_q_lat=g_q_lat[l][None], w_uq=wq.astype(BF16), g_kv_lat=g_kv_lat[l][None],
        w_k=wk.astype(BF16), w_v=wv.astype(BF16), gq=gq[None], gk=gk[None],
        conv_w=conv_w[l], conv_b=conv_b[l][None], w_gate=w_gate.astype(BF16), b_gate=b_gate[None],
        lam=lru_lambda[l], w_pool=_block_diag(w_pool[l]).astype(BF16), pool_scale=pool_scale[l][None],
        w_out=w_out[l].astype(BF16), w_ff1=w_ff1[l].astype(BF16), w_ff2=w_ff2[l].astype(BF16))


def _rope_tables(T):
    partner = _rope_partner()
    j = np.arange(MLA_ROPE)
    axis = j // (2 * ROPE_FREQS)
    half = (j % (2 * ROPE_FREQS)) // ROPE_FREQS
    f = j % ROPE_FREQS
    del partner
    t = jnp.arange(T)
    pos = jnp.stack([(t // GRID_W).astype(F32), (t % GRID_W).astype(F32)], axis=-1)
    freqs = jnp.power(ROPE_BASE, -jnp.arange(ROPE_FREQS, dtype=F32) / ROPE_FREQS)
    ang = pos[:, axis] * freqs[f]
    sign = jnp.asarray(np.where(half == 0, -1.0, 1.0), F32)
    ct = jnp.concatenate([jnp.ones((T, MLA_NOPE), F32), jnp.cos(ang),
                          jnp.zeros((T, HEAD_PAD - MLA_QK), F32)], axis=1)
    st = jnp.concatenate([jnp.zeros((T, MLA_NOPE), F32), jnp.sin(ang) * sign,
                          jnp.zeros((T, HEAD_PAD - MLA_QK), F32)], axis=1)
    return ct, st


def _no_rope_tables(T):
    lane = np.arange(HEAD_PAD)
    ct = jnp.broadcast_to(jnp.asarray((lane < MLA_QK).astype(np.float32)), (T, HEAD_PAD))
    return ct, jnp.zeros((T, HEAD_PAD), F32)


def _tile(n, pref):
    t = min(n, pref)
    while n % t:
        t //= 2
    return t


def kernel(x, c, ctx, c_ctx, w_mod, b_mod, g_norm1, g_norm2, w_in, conv_w, conv_b, lru_w_a, lru_b_a,
           lru_w_x, lru_b_x, lru_lambda, g_q_lat, w_uq, g_kv_lat, w_ukv, g_qn, g_kn, w_pool, pool_scale,
           w_out, w_ff1, w_ff2):
    B, L, D = x.shape
    Lc = ctx.shape[1]
    depth = w_mod.shape[0]
    d_lru = conv_w.shape[2]

    rows = -(-(B + 1) // SUBLANES) * SUBLANES
    cvec = jnp.zeros((rows, D), F32).at[:B].set(c).at[B].set(c_ctx)
    mods = _mod_call(cvec, w_mod, b_mod)

    rope_l = _rope_tables(L)
    rope_c = _no_rope_tables(Lc)
    tm_l, tm_c = _tile(L, 256), _tile(Lc, 256)
    tp_l, tp_c = _tile(L, 512), _tile(Lc, 512)
    tq_l, tq_c = _tile(L, 256), _tile(Lc, 256)
    tc_l, tc_c = _tile(L, 256), _tile(Lc, 256)

    h = ctx
    zeros_state = jnp.zeros((B, 2, d_lru), F32)
    for l in range(depth):
        last = l == depth - 1
        lw = _layer_weights(l, w_in, conv_w, conv_b, lru_w_a, lru_b_a, lru_w_x, lru_b_x, lru_lambda,
                            g_q_lat, w_uq, g_kv_lat, w_ukv, g_qn, g_kn, w_pool, pool_scale, w_out,
                            w_ff1, w_ff2)
        ml = mods[l, :B].reshape(B, 1, 6, D)
        mc = jnp.broadcast_to(mods[l, B].reshape(1, 1, 6, D), (B, 1, 6, D))
        sh1, sc1, g1, sh2, sc2, g2 = (ml[:, :, i] for i in range(6))
        csh1, csc1, cg1, csh2, csc2, cg2 = (mc[:, :, i] for i in range(6))
        gn1 = g_norm1[l][None]
        gn2 = g_norm2[l][None]

        lx_c, lg_c, pu_c, q_c, k_c, v_c = _inproj_call(h, csh1, csc1, gn1, lw, *rope_c, tm_c)
        a_c, p_c, hfin = _seq_call(lx_c, lg_c, pu_c, zeros_state, lw, tc_c)
        lx_l, lg_l, pu_l, q_l, k_l, v_l = _inproj_call(x, sh1, sc1, gn1, lw, *rope_l, tm_l)
        a_l, p_l, _ = _seq_call(lx_l, lg_l, pu_l, hfin, lw, tc_l)
        att_l = _attn_call(q_l, [(k_c, v_c), (k_l, v_l)], tq_l)
        x = _post_call(x, a_l, att_l, p_l, g1, sh2, sc2, g2, gn2, lw, tp_l)
        if not last:
            att_c = _attn_call(q_c, [(k_c, v_c)], tq_c)
            h = _post_call(h, a_c, att_c, p_c, cg1, csh2, csc2, cg2, gn2, lw, tp_c)
    return x
```

```python
import functools
import math

import jax
import jax.numpy as jnp
import numpy as np
from jax import lax
from jax.experimental import pallas as pl
from jax.experimental.pallas import tpu as pltpu

F32 = jnp.float32
BF16 = jnp.bfloat16

EPS = 1e-6
GRID_W = 64
LRU_HEADS = 4
LRU_C = 8.0
CONV_W = 4
MLA_V = 64
MLA_NOPE = 64
MLA_ROPE = 32
MLA_QK = MLA_NOPE + MLA_ROPE
ROPE_FREQS = MLA_ROPE // 4
ROPE_BASE = 10000.0
POOL_WINDOWS = (2, 4, 8, 16)

LANES = 128
SUBLANES = 8
HEAD_PAD = LANES
HALO = 16
VMEM_LIMIT = 56 * 1024 * 1024

LOG2E = 1.4426950408889634


def _cparams(n_grid):
    return pltpu.CompilerParams(dimension_semantics=("arbitrary",) * n_grid,
                                vmem_limit_bytes=VMEM_LIMIT)


def _const_spec(shape):
    zeros = (0,) * len(shape)
    return pl.BlockSpec(shape, lambda *_: zeros)


def _mod_kernel(c_ref, w_ref, b_ref, o_ref):
    c = c_ref[...]
    act = (c * jax.nn.sigmoid(c)).astype(BF16)
    o_ref[0] = jnp.dot(act, w_ref[0].astype(BF16), preferred_element_type=F32) + b_ref[0]


def _mod_call(cvec, w_mod, b_mod):
    depth, d, n = w_mod.shape
    rows = cvec.shape[0]
    tn = 1024
    return pl.pallas_call(
        _mod_kernel,
        grid=(depth, n // tn),
        in_specs=[pl.BlockSpec((rows, d), lambda l, j: (0, 0)),
                  pl.BlockSpec((1, d, tn), lambda l, j: (l, 0, j)),
                  pl.BlockSpec((1, 1, tn), lambda l, j: (l, 0, j))],
        out_specs=pl.BlockSpec((1, rows, tn), lambda l, j: (l, 0, j)),
        out_shape=jax.ShapeDtypeStruct((depth, rows, n), F32),
        compiler_params=_cparams(2),
        name="adaln_mod",
    )(cvec, w_mod, b_mod.reshape(depth, 1, n))


def _inproj_kernel(x_ref, sh_ref, sc_ref, g1_ref, win_ref, gql_ref, wuq_ref, gkvl_ref, wk_ref,
                   wv_ref, gq_ref, gk_ref, ct_ref, st_ref,
                   lx_ref, lg_ref, pu_ref, q_ref, k_ref, v_ref, *, d_lru, q_rank, kv_rank, heads):
    x = x_ref[0]
    xn = x * lax.rsqrt(jnp.mean(x * x, axis=-1, keepdims=True) + EPS)
    hx = (xn * g1_ref[...]) * (1.0 + sc_ref[0]) + sh_ref[0]
    z = jnp.dot(hx.astype(BF16), win_ref[...], preferred_element_type=F32)
    o = 0
    lx_ref[0] = z[:, o:o + d_lru]; o += d_lru
    lg_ref[0] = z[:, o:o + d_lru]; o += d_lru
    ql = z[:, o:o + q_rank]; o += q_rank
    kvl = z[:, o:o + kv_rank]; o += kv_rank
    krb = z[:, o:o + HEAD_PAD]; o += HEAD_PAD
    pu_ref[0] = z[:, o:]

    qln = ql * lax.rsqrt(jnp.mean(ql * ql, axis=-1, keepdims=True) + EPS) * gql_ref[...]
    qf = jnp.dot(qln.astype(BF16), wuq_ref[...], preferred_element_type=F32)
    kvn = kvl * lax.rsqrt(jnp.mean(kvl * kvl, axis=-1, keepdims=True) + EPS) * gkvl_ref[...]
    kvn = kvn.astype(BF16)
    kf = jnp.dot(kvn, wk_ref[...], preferred_element_type=F32)
    vf = jnp.dot(kvn, wv_ref[...], preferred_element_type=F32)

    lane = lax.broadcasted_iota(jnp.int32, (1, HEAD_PAD), 1)
    real = (lane < MLA_QK).astype(F32)
    ct = ct_ref[...]
    st = st_ref[...]
    gq = gq_ref[...]
    gk = gk_ref[...]
    inv_qk = 1.0 / MLA_QK
    for h in range(heads):
        qh = qf[:, h * HEAD_PAD:(h + 1) * HEAD_PAD]
        ss = jnp.sum(qh * qh * real, axis=-1, keepdims=True) * inv_qk
        qh = qh * lax.rsqrt(ss + EPS) * gq
        qh = qh * ct + pltpu.roll(qh, HEAD_PAD - MLA_ROPE, axis=1) * st
        q_ref[0, h] = qh.astype(BF16)
        kh = kf[:, h * HEAD_PAD:(h + 1) * HEAD_PAD] + krb
        ss = jnp.sum(kh * kh * real, axis=-1, keepdims=True) * inv_qk
        kh = kh * lax.rsqrt(ss + EPS) * gk
        kh = kh * ct + pltpu.roll(kh, HEAD_PAD - MLA_ROPE, axis=1) * st
        k_ref[0, h] = kh.astype(BF16)
    for p in range(heads // 2):
        v_ref[0, p] = vf[:, p * LANES:(p + 1) * LANES].T.astype(BF16)


def _inproj_call(x, sh, sc, g1, lw, ct, st, tm):
    B, T, D = x.shape
    heads = lw["heads"]
    d_lru, q_rank, kv_rank = lw["d_lru"], lw["q_rank"], lw["kv_rank"]
    d_pool = lw["w_in"].shape[1] - (2 * d_lru + q_rank + kv_rank + HEAD_PAD)
    tok = lambda w: pl.BlockSpec((1, tm, w), lambda b, i: (b, i, 0))
    vec = pl.BlockSpec((1, 1, D), lambda b, i: (b, 0, 0))
    hd = lambda n: pl.BlockSpec((1, n, tm, LANES), lambda b, i: (b, 0, i, 0))
    kern = functools.partial(_inproj_kernel, d_lru=d_lru, q_rank=q_rank, kv_rank=kv_rank, heads=heads)
    return pl.pallas_call(
        kern,
        grid=(B, T // tm),
        in_specs=[tok(D), vec, vec, _const_spec((1, D)), _const_spec(lw["w_in"].shape),
                  _const_spec((1, q_rank)), _const_spec(lw["w_uq"].shape),
                  _const_spec((1, kv_rank)), _const_spec(lw["w_k"].shape), _const_spec(lw["w_v"].shape),
                  _const_spec((1, HEAD_PAD)), _const_spec((1, HEAD_PAD)),
                  pl.BlockSpec((tm, HEAD_PAD), lambda b, i: (i, 0)),
                  pl.BlockSpec((tm, HEAD_PAD), lambda b, i: (i, 0))],
        out_specs=[tok(d_lru), tok(d_lru), tok(d_pool), hd(heads), hd(heads),
                   pl.BlockSpec((1, heads // 2, LANES, tm), lambda b, i: (b, 0, 0, i))],
        out_shape=[jax.ShapeDtypeStruct((B, T, d_lru), F32),
                   jax.ShapeDtypeStruct((B, T, d_lru), F32),
                   jax.ShapeDtypeStruct((B, T, d_pool), F32),
                   jax.ShapeDtypeStruct((B, heads, T, LANES), BF16),
                   jax.ShapeDtypeStruct((B, heads, T, LANES), BF16),
                   jax.ShapeDtypeStruct((B, heads // 2, LANES, T), BF16)],
        compiler_params=_cparams(2),
        name="in_proj",
    )(x, sh, sc, g1, lw["w_in"], lw["g_q_lat"], lw["w_uq"], lw["g_kv_lat"], lw["w_k"], lw["w_v"],
      lw["gq"], lw["gk"], ct, st)


def _roll_rows(a, shift):
    return pltpu.roll(a, shift % a.shape[0], axis=0)


def _scan8(a, b, reverse):
    n = a.shape[0]
    a3 = a.reshape(n // SUBLANES, SUBLANES, a.shape[1])
    b3 = b.reshape(n // SUBLANES, SUBLANES, b.shape[1])
    rid = lax.broadcasted_iota(jnp.int32, a3.shape, 1)
    for s in (1, 2, 4):
        if reverse:
            m = rid < SUBLANES - s
            sh = SUBLANES - s
        else:
            m = rid >= s
            sh = s
        a_sh = pltpu.roll(a3, sh, axis=1)
        b_sh = pltpu.roll(b3, sh, axis=1)
        b3 = jnp.where(m, a3 * b_sh + b3, b3)
        a3 = jnp.where(m, a3 * a_sh, a3)
    return a3, b3


def _seq_kernel(lx_ref, lg_ref, pu_ref, h0_ref, cw_ref, cb_ref, wg_ref, bg_ref, lam_ref, wp_ref,
                ps_ref, a_out_ref, p_out_ref, hfin_ref, xpad_ref, upad_ref, hf_ref, *, tc):
    T = lx_ref.shape[1]
    C = lx_ref.shape[2]
    nc = T // tc
    zeros_halo = jnp.zeros((HALO, C), F32)
    xpad_ref[0:HALO, :] = zeros_halo
    xpad_ref[HALO + T:HALO + T + HALO, :] = zeros_halo
    upad_ref[0:HALO, :] = zeros_halo
    upad_ref[HALO + T:HALO + T + HALO, :] = zeros_halo

    def fill(c, carry):
        r0 = pl.multiple_of(c * tc, tc)
        xpad_ref[pl.ds(HALO + r0, tc), :] = lx_ref[0, pl.ds(r0, tc), :]
        upad_ref[pl.ds(HALO + r0, tc), :] = pu_ref[0, pl.ds(r0, tc), :]
        return carry

    lax.fori_loop(0, nc, fill, 0)

    lam = lam_ref[...]
    coef = -LRU_C * (jnp.maximum(-lam, 0.0) + jnp.log(1.0 + jnp.exp(-jnp.abs(lam))))
    cw = cw_ref[...]
    cb = cb_ref[...]
    bg = bg_ref[...]

    def conv_chunk(r0):
        w = xpad_ref[pl.ds(r0 + HALO - SUBLANES, tc + 2 * SUBLANES), :]
        y = (cw[0:1] * _roll_rows(w, 1) + cw[1:2] * w + cw[2:3] * _roll_rows(w, -1)
             + cw[3:4] * _roll_rows(w, -2))
        return y[SUBLANES:SUBLANES + tc] + cb

    def coeffs(xc, d):
        g = jnp.dot(xc.astype(BF16), wg_ref[:, 2 * C * d:2 * C * (d + 1)],
                    preferred_element_type=F32) + bg[:, 2 * C * d:2 * C * (d + 1)]
        r = jax.nn.sigmoid(g[:, :C])
        i = jax.nn.sigmoid(g[:, C:])
        log_a = coef[d:d + 1] * r
        a = jnp.exp(log_a)
        b = jnp.sqrt(1.0 - a * a) * (i * xc)
        return a, b

    lane = lax.broadcasted_iota(jnp.int32, (1, C), 1)
    gdim = C // len(POOL_WINDOWS)

    def pool_chunk(r0):
        w = upad_ref[pl.ds(r0, tc + 2 * HALO), :]
        s2 = w + _roll_rows(w, 1)
        s4 = _roll_rows(s2, 1) + _roll_rows(s2, -1)
        s8 = _roll_rows(s4, 2) + _roll_rows(s4, -2)
        s16 = _roll_rows(s8, 4) + _roll_rows(s8, -4)
        t = r0 + lax.broadcasted_iota(jnp.int32, (tc, 1), 0)
        sel = None
        cnt = None
        for g, (win, s) in enumerate(zip(POOL_WINDOWS, (s2, s4, s8, s16))):
            sg = s[HALO:HALO + tc]
            cg = (jnp.minimum(t - win // 2 + win, T) - jnp.maximum(t - win // 2, 0)).astype(F32)
            if sel is None:
                sel, cnt = sg, jnp.broadcast_to(cg, (tc, C))
            else:
                m = lane >= g * gdim
                sel = jnp.where(m, sg, sel)
                cnt = jnp.where(m, cg, cnt)
        mixed = sel / cnt - w[HALO:HALO + tc]
        y = jnp.dot(mixed.astype(BF16), wp_ref[...], preferred_element_type=F32) * ps_ref[...]
        p_out_ref[0, pl.ds(r0, tc), :] = y.astype(p_out_ref.dtype)

    nv = tc // SUBLANES

    def fwd(c, h):
        r0 = pl.multiple_of(c * tc, tc)
        pool_chunk(r0)
        a, b = coeffs(conv_chunk(r0), 0)
        a3, b3 = _scan8(a, b, reverse=False)
        for j in range(nv):
            hj = b3[j] + a3[j] * h
            hf_ref[pl.ds(r0 + j * SUBLANES, SUBLANES), :] = hj
            h = jnp.broadcast_to(hj[SUBLANES - 1:SUBLANES], (SUBLANES, C))
        return h

    h = lax.fori_loop(0, nc, fwd, jnp.broadcast_to(h0_ref[0, 0:1, :], (SUBLANES, C)))
    hfin_ref[0, 0:1, :] = h[0:1]

    def bwd(cc, h):
        c = nc - 1 - cc
        r0 = pl.multiple_of(c * tc, tc)
        a, b = coeffs(conv_chunk(r0), 1)
        a3, b3 = _scan8(a, b, reverse=True)
        for j in range(nv - 1, -1, -1):
            hj = b3[j] + a3[j] * h
            rows = pl.ds(r0 + j * SUBLANES, SUBLANES)
            y = hj + hf_ref[rows, :]
            a_out_ref[0, rows, :] = (y * jax.nn.gelu(lg_ref[0, rows, :])).astype(a_out_ref.dtype)
            h = jnp.broadcast_to(hj[0:1], (SUBLANES, C))
        return h

    h = lax.fori_loop(0, nc, bwd, jnp.broadcast_to(h0_ref[0, 1:2, :], (SUBLANES, C)))
    hfin_ref[0, 1:2, :] = h[0:1]


def _seq_call(lx, lg, pu, h0, lw, tc):
    B, T, C = lx.shape
    seq = pl.BlockSpec((1, T, C), lambda b: (b, 0, 0))
    st = pl.BlockSpec((1, 2, C), lambda b: (b, 0, 0))
    return pl.pallas_call(
        functools.partial(_seq_kernel, tc=tc),
        grid=(B,),
        in_specs=[seq, seq, seq, st, _const_spec((CONV_W, C)), _const_spec((1, C)),
                  _const_spec((C, 4 * C)), _const_spec((1, 4 * C)), _const_spec((2, C)),
                  _const_spec((C, C)), _const_spec((1, C))],
        out_specs=[seq, seq, st],
        out_shape=[jax.ShapeDtypeStruct((B, T, C), BF16), jax.ShapeDtypeStruct((B, T, C), BF16),
                   jax.ShapeDtypeStruct((B, 2, C), F32)],
        scratch_shapes=[pltpu.VMEM((T + 2 * HALO, C), F32), pltpu.VMEM((T + 2 * HALO, C), F32),
                        pltpu.VMEM((T, C), F32)],
        compiler_params=_cparams(1),
        name="seq_mix",
    )(lx, lg, pu, h0, lw["conv_w"], lw["conv_b"], lw["w_gate"], lw["b_gate"], lw["lam"],
      lw["w_pool"], lw["pool_scale"])


def _attn_kernel(*refs, n_seg, tq_sub, kc):
    q_ref = refs[0]
    kv_refs = refs[1:1 + 2 * n_seg]
    o_ref = refs[1 + 2 * n_seg]
    s_scr = refs[2 + 2 * n_seg]
    n_sub = q_ref.shape[2] // tq_sub
    units = [(hh, j) for j in range(n_sub) for hh in range(2)]
    chunks = []
    off = 0
    for seg in range(n_seg):
        lk = kv_refs[2 * seg].shape[2]
        for st in range(0, lk, kc):
            sz = min(kc, lk - st)
            chunks.append((seg, st, sz, off))
            off += sz
    col_max = {}
    results = {}

    def scores(u, slot):
        hh, j = units[u]
        q = q_ref[0, hh, j * tq_sub:(j + 1) * tq_sub, :]
        mvec = None
        for seg, st, sz, off in chunks:
            k = kv_refs[2 * seg][0, hh, st:st + sz, :]
            s = lax.dot_general(k, q, (((1,), (1,)), ((), ())), preferred_element_type=F32)
            s_scr[slot, off:off + sz, :] = s
            cm = jnp.max(s.reshape(sz // SUBLANES, SUBLANES, tq_sub), axis=0)
            mvec = cm if mvec is None else jnp.maximum(mvec, cm)
            yield
        col_max[u] = jnp.max(mvec, axis=0, keepdims=True)

    def weighted(u, slot):
        hh, j = units[u]
        m = col_max[u]
        lvec = None
        acc = None
        for seg, st, sz, off in chunks:
            p = jnp.exp2(s_scr[slot, off:off + sz, :] - m)
            ls = jnp.sum(p.reshape(sz // SUBLANES, SUBLANES, tq_sub), axis=0)
            lvec = ls if lvec is None else lvec + ls
            vt = kv_refs[2 * seg + 1][0, 0, hh * MLA_V:(hh + 1) * MLA_V, st:st + sz]
            pv = jnp.dot(vt, p.astype(BF16), preferred_element_type=F32)
            acc = pv if acc is None else acc + pv
            yield
        results[units[u]] = acc / jnp.sum(lvec, axis=0, keepdims=True)

    for _ in scores(0, 0):
        pass
    for u in range(len(units)):
        second = weighted(u, u % 2)
        first = scores(u + 1, (u + 1) % 2) if u + 1 < len(units) else iter(())
        for _ in second:
            next(first, None)
        for _ in first:
            pass

    for j in range(n_sub):
        o_t = jnp.concatenate([results[(0, j)], results[(1, j)]], axis=0)
        o_ref[0, j * tq_sub:(j + 1) * tq_sub, :] = o_t.T.astype(o_ref.dtype)


def _attn_call(q, segs, tq):
    B, H, T, _ = q.shape
    tq_sub = min(tq, 256)
    in_specs = [pl.BlockSpec((1, 2, tq, LANES), lambda b, p, i: (b, p, i, 0))]
    args = [q]
    lk_total = 0
    for k, v in segs:
        lk = k.shape[2]
        lk_total += lk
        in_specs.append(pl.BlockSpec((1, 2, lk, LANES), lambda b, p, i: (b, p, 0, 0)))
        in_specs.append(pl.BlockSpec((1, 1, LANES, lk), lambda b, p, i: (b, p, 0, 0)))
        args += [k, v]
    return pl.pallas_call(
        functools.partial(_attn_kernel, n_seg=len(segs), tq_sub=tq_sub, kc=256),
        grid=(B, H // 2, T // tq),
        in_specs=in_specs,
        out_specs=pl.BlockSpec((1, tq, LANES), lambda b, p, i: (b, i, p)),
        out_shape=jax.ShapeDtypeStruct((B, T, H * MLA_V), BF16),
        scratch_shapes=[pltpu.VMEM((2, lk_total, tq_sub), F32)],
        compiler_params=_cparams(3),
        name="attention",
    )(*args)


def _post_kernel(x_ref, a_ref, att_ref, p_ref, g1_ref, sh_ref, sc_ref, g2_ref, gn_ref, wo_ref,
                 w1_ref, w2_ref, o_ref, *, ff_chunk):
    d_lru = a_ref.shape[2]
    d_att = att_ref.shape[2]
    y = jnp.dot(a_ref[0], wo_ref[0:d_lru, :], preferred_element_type=F32)
    y += jnp.dot(att_ref[0], wo_ref[d_lru:d_lru + d_att, :], preferred_element_type=F32)
    y += jnp.dot(p_ref[0], wo_ref[d_lru + d_att:, :], preferred_element_type=F32)
    x1 = x_ref[0] + g1_ref[0] * y
    xn = x1 * lax.rsqrt(jnp.mean(x1 * x1, axis=-1, keepdims=True) + EPS)
    h = ((xn * gn_ref[...]) * (1.0 + sc_ref[0]) + sh_ref[0]).astype(BF16)
    d_ff = w1_ref.shape[1]
    acc = None
    for c in range(d_ff // ff_chunk):
        u = jnp.dot(h, w1_ref[:, c * ff_chunk:(c + 1) * ff_chunk], preferred_element_type=F32)
        u = jnp.square(jnp.maximum(u, 0.0)).astype(BF16)
        part = jnp.dot(u, w2_ref[c * ff_chunk:(c + 1) * ff_chunk, :], preferred_element_type=F32)
        acc = part if acc is None else acc + part
    o_ref[0] = x1 + g2_ref[0] * acc


def _post_call(x, a, att, p, g1, sh, sc, g2, gn, lw, tm):
    B, T, D = x.shape
    tok = lambda w: pl.BlockSpec((1, tm, w), lambda b, i: (b, i, 0))
    vec = pl.BlockSpec((1, 1, D), lambda b, i: (b, 0, 0))
    single = lambda shape: pl.BlockSpec(shape, lambda b, i: (0,) * len(shape),
                                        pipeline_mode=pl.Buffered(1))
    return pl.pallas_call(
        functools.partial(_post_kernel, ff_chunk=1024),
        grid=(B, T // tm),
        in_specs=[tok(D), tok(a.shape[2]), tok(att.shape[2]), tok(p.shape[2]), vec, vec, vec, vec,
                  _const_spec((1, D)), single(lw["w_out"].shape), single(lw["w_ff1"].shape),
                  single(lw["w_ff2"].shape)],
        out_specs=tok(D),
        out_shape=jax.ShapeDtypeStruct((B, T, D), F32),
        compiler_params=_cparams(2),
        name="out_proj_mlp",
    )(x, a, att, p, g1, sh, sc, g2, gn, lw["w_out"], lw["w_ff1"], lw["w_ff2"])


def _rope_partner():
    j = np.arange(MLA_ROPE)
    half = (j % (2 * ROPE_FREQS)) // ROPE_FREQS
    return np.where(half == 0, j + ROPE_FREQS, j - ROPE_FREQS)


def _block_diag(w):
    n, a, b = w.shape
    out = jnp.zeros((n * a, n * b), w.dtype)
    for i in range(n):
        out = out.at[i * a:(i + 1) * a, i * b:(i + 1) * b].set(w[i])
    return out


def _layer_weights(l, w_in, conv_w, conv_b, lru_w_a, lru_b_a, lru_w_x, lru_b_x, lru_lambda, g_q_lat,
                   w_uq, g_kv_lat, w_ukv, g_qn, g_kn, w_pool, pool_scale, w_out, w_ff1, w_ff2):
    d = w_in.shape[1]
    d_lru = conv_w.shape[2]
    q_rank = w_uq.shape[1]
    kv_rank = w_ukv.shape[1]
    heads = w_uq.shape[2] // MLA_QK
    partner = _rope_partner()
    o_kr = 2 * d_lru + q_rank + kv_rank
    wi = w_in[l]
    kr = wi[:, o_kr:o_kr + MLA_ROPE]
    krb = jnp.concatenate([jnp.zeros((d, MLA_NOPE), F32), kr, kr[:, partner]], axis=1)
    w_in_p = jnp.concatenate([wi[:, :o_kr], krb, wi[:, o_kr + MLA_ROPE:]], axis=1).astype(BF16)
    wq = w_uq[l].reshape(q_rank, heads, MLA_QK)
    wq = jnp.concatenate([wq, wq[:, :, MLA_NOPE + partner]], axis=2).reshape(q_rank, heads * HEAD_PAD)
    wkv = w_ukv[l].reshape(kv_rank, heads, MLA_NOPE + MLA_V)
    wk = jnp.concatenate([wkv[:, :, :MLA_NOPE], jnp.zeros((kv_rank, heads, HEAD_PAD - MLA_NOPE), F32)],
                         axis=2).reshape(kv_rank, heads * HEAD_PAD)
    wv = wkv[:, :, MLA_NOPE:].reshape(kv_rank, heads * MLA_V)
    gq = jnp.concatenate([g_qn[l], g_qn[l][MLA_NOPE + partner]]) * (LOG2E / math.sqrt(MLA_QK))
    gk = jnp.concatenate([g_kn[l], g_kn[l][MLA_NOPE + partner]])
    w_gate = jnp.concatenate([_block_diag(lru_w_a[l, 0]), _block_diag(lru_w_x[l, 0]),
                              _block_diag(lru_w_a[l, 1]), _block_diag(lru_w_x[l, 1])], axis=1)
    b_gate = jnp.concatenate([lru_b_a[l, 0], lru_b_x[l, 0], lru_b_a[l, 1], lru_b_x[l, 1]])
    return dict(
        heads=heads, d_lru=d_lru, q_rank=q_rank, kv_rank=kv_rank,
        w_in=w_in_p, g_q_lat=g_q_lat[l][None], w_uq=wq.astype(BF16), g_kv_lat=g_kv_lat[l][None],
        w_k=wk.astype(BF16), w_v=wv.astype(BF16), gq=gq[None], gk=gk[None],
        conv_w=conv_w[l], conv_b=conv_b[l][None], w_gate=w_gate.astype(BF16), b_gate=b_gate[None],
        lam=lru_lambda[l], w_pool=_block_diag(w_pool[l]).astype(BF16), pool_scale=pool_scale[l][None],
        w_out=w_out[l].astype(BF16), w_ff1=w_ff1[l].astype(BF16), w_ff2=w_ff2[l].astype(BF16))


def _rope_tables(T):
    partner = _rope_partner()
    j = np.arange(MLA_ROPE)
    axis = j // (2 * ROPE_FREQS)
    half = (j % (2 * ROPE_FREQS)) // ROPE_FREQS
    f = j % ROPE_FREQS
    del partner
    t = jnp.arange(T)
    pos = jnp.stack([(t // GRID_W).astype(F32), (t % GRID_W).astype(F32)], axis=-1)
    freqs = jnp.power(ROPE_BASE, -jnp.arange(ROPE_FREQS, dtype=F32) / ROPE_FREQS)
    ang = pos[:, axis] * freqs[f]
    sign = jnp.asarray(np.where(half == 0, -1.0, 1.0), F32)
    ct = jnp.concatenate([jnp.ones((T, MLA_NOPE), F32), jnp.cos(ang),
                          jnp.zeros((T, HEAD_PAD - MLA_QK), F32)], axis=1)
    st = jnp.concatenate([jnp.zeros((T, MLA_NOPE), F32), jnp.sin(ang) * sign,
                          jnp.zeros((T, HEAD_PAD - MLA_QK), F32)], axis=1)
    return ct, st


def _no_rope_tables(T):
    lane = np.arange(HEAD_PAD)
    ct = jnp.broadcast_to(jnp.asarray((lane < MLA_QK).astype(np.float32)), (T, HEAD_PAD))
    return ct, jnp.zeros((T, HEAD_PAD), F32)


def _tile(n, pref):
    t = min(n, pref)
    while n % t:
        t //= 2
    return t


def kernel(x, c, ctx, c_ctx, w_mod, b_mod, g_norm1, g_norm2, w_in, conv_w, conv_b, lru_w_a, lru_b_a,
           lru_w_x, lru_b_x, lru_lambda, g_q_lat, w_uq, g_kv_lat, w_ukv, g_qn, g_kn, w_pool, pool_scale,
           w_out, w_ff1, w_ff2):
    B, L, D = x.shape
    Lc = ctx.shape[1]
    depth = w_mod.shape[0]
    d_lru = conv_w.shape[2]

    rows = -(-(B + 1) // SUBLANES) * SUBLANES
    cvec = jnp.zeros((rows, D), F32).at[:B].set(c).at[B].set(c_ctx)
    mods = _mod_call(cvec, w_mod, b_mod)

    rope_l = _rope_tables(L)
    rope_c = _no_rope_tables(Lc)
    tm_l, tm_c = _tile(L, 256), _tile(Lc, 256)
    tp_l, tp_c = _tile(L, 512), _tile(Lc, 512)
    tq_l, tq_c = _tile(L, 512), _tile(Lc, 512)
    tc_l, tc_c = _tile(L, 256), _tile(Lc, 256)

    h = ctx
    zeros_state = jnp.zeros((B, 2, d_lru), F32)
    for l in range(depth):
        last = l == depth - 1
        lw = _layer_weights(l, w_in, conv_w, conv_b, lru_w_a, lru_b_a, lru_w_x, lru_b_x, lru_lambda,
                            g_q_lat, w_uq, g_kv_lat, w_ukv, g_qn, g_kn, w_pool, pool_scale, w_out,
                            w_ff1, w_ff2)
        ml = mods[l, :B].reshape(B, 1, 6, D)
        mc = jnp.broadcast_to(mods[l, B].reshape(1, 1, 6, D), (B, 1, 6, D))
        sh1, sc1, g1, sh2, sc2, g2 = (ml[:, :, i] for i in range(6))
        csh1, csc1, cg1, csh2, csc2, cg2 = (mc[:, :, i] for i in range(6))
        gn1 = g_norm1[l][None]
        gn2 = g_norm2[l][None]

        lx_c, lg_c, pu_c, q_c, k_c, v_c = _inproj_call(h, csh1, csc1, gn1, lw, *rope_c, tm_c)
        a_c, p_c, hfin = _seq_call(lx_c, lg_c, pu_c, zeros_state, lw, tc_c)
        lx_l, lg_l, pu_l, q_l, k_l, v_l = _inproj_call(x, sh1, sc1, gn1, lw, *rope_l, tm_l)
        a_l, p_l, _ = _seq_call(lx_l, lg_l, pu_l, hfin, lw, tc_l)
        att_l = _attn_call(q_l, [(k_c, v_c), (k_l, v_l)], tq_l)
        x = _post_call(x, a_l, att_l, p_l, g1, sh2, sc2, g2, gn2, lw, tp_l)
        if not last:
            att_c = _attn_call(q_c, [(k_c, v_c)], tq_c)
            h = _post_call(h, a_c, att_c, p_c, cg1, csh2, csc2, cg2, gn2, lw, tp_c)
    return x
```

```python
import functools
import math

import jax
import jax.numpy as jnp
import numpy as np
from jax import lax
from jax.experimental import pallas as pl
from jax.experimental.pallas import tpu as pltpu

F32 = jnp.float32
BF16 = jnp.bfloat16

EPS = 1e-6
GRID_W = 64
LRU_HEADS = 4
LRU_C = 8.0
CONV_W = 4
MLA_V = 64
MLA_NOPE = 64
MLA_ROPE = 32
MLA_QK = MLA_NOPE + MLA_ROPE
ROPE_FREQS = MLA_ROPE // 4
ROPE_BASE = 10000.0
POOL_WINDOWS = (2, 4, 8, 16)

LANES = 128
SUBLANES = 8
HEAD_PAD = LANES
BF16_ROWS = 16
V_ROWS = MLA_V + BF16_ROWS
HALO = 16
VMEM_LIMIT = 56 * 1024 * 1024

LOG2E = 1.4426950408889634


def _cparams(n_grid, flags=None):
    return pltpu.CompilerParams(dimension_semantics=("arbitrary",) * n_grid,
                                vmem_limit_bytes=VMEM_LIMIT, flags=flags)


def _const_spec(shape):
    zeros = (0,) * len(shape)
    return pl.BlockSpec(shape, lambda *_: zeros)


def _mod_kernel(c_ref, w_ref, b_ref, o_ref):
    c = c_ref[...]
    act = (c * jax.nn.sigmoid(c)).astype(BF16)
    o_ref[0] = jnp.dot(act, w_ref[0].astype(BF16), preferred_element_type=F32) + b_ref[0]


def _mod_call(cvec, w_mod, b_mod):
    depth, d, n = w_mod.shape
    rows = cvec.shape[0]
    tn = 1024
    return pl.pallas_call(
        _mod_kernel,
        grid=(depth, n // tn),
        in_specs=[pl.BlockSpec((rows, d), lambda l, j: (0, 0)),
                  pl.BlockSpec((1, d, tn), lambda l, j: (l, 0, j)),
                  pl.BlockSpec((1, 1, tn), lambda l, j: (l, 0, j))],
        out_specs=pl.BlockSpec((1, rows, tn), lambda l, j: (l, 0, j)),
        out_shape=jax.ShapeDtypeStruct((depth, rows, n), F32),
        compiler_params=_cparams(2),
        name="adaln_mod",
    )(cvec, w_mod, b_mod.reshape(depth, 1, n))


def _inproj_kernel(x_ref, sh_ref, sc_ref, g1_ref, win_ref, gql_ref, wuq_ref, gkvl_ref, wk_ref,
                   wv_ref, ones_ref, aq_ref, bq_ref, ak_ref, bk_ref,
                   lx_ref, lg_ref, pu_ref, q_ref, k_ref, v_ref, *, d_lru, q_rank, kv_rank, heads):
    x = x_ref[0]
    xn = x * lax.rsqrt(jnp.mean(x * x, axis=-1, keepdims=True) + EPS)
    hx = (xn * g1_ref[...]) * (1.0 + sc_ref[0]) + sh_ref[0]
    z = jnp.dot(hx.astype(BF16), win_ref[...], preferred_element_type=F32)
    o = 0
    lx_ref[0] = z[:, o:o + d_lru]; o += d_lru
    lg_ref[0] = z[:, o:o + d_lru]; o += d_lru
    ql = z[:, o:o + q_rank]; o += q_rank
    kvl = z[:, o:o + kv_rank]; o += kv_rank
    krb = z[:, o:o + HEAD_PAD]; o += HEAD_PAD
    pu_ref[0] = z[:, o:]

    qln = ql * lax.rsqrt(jnp.mean(ql * ql, axis=-1, keepdims=True) + EPS) * gql_ref[...]
    qf = jnp.dot(qln.astype(BF16), wuq_ref[...], preferred_element_type=F32)
    kvn = kvl * lax.rsqrt(jnp.mean(kvl * kvl, axis=-1, keepdims=True) + EPS) * gkvl_ref[...]
    kvn = kvn.astype(BF16)
    kf = jnp.dot(kvn, wk_ref[...], preferred_element_type=F32)
    vf = jnp.dot(kvn, wv_ref[...], preferred_element_type=F32)

    lane = lax.broadcasted_iota(jnp.int32, (1, HEAD_PAD), 1)
    kr = jnp.where((lane >= MLA_NOPE) & (lane < MLA_QK), krb, 0.0)
    kr_sw = pltpu.roll(krb, HEAD_PAD - MLA_ROPE, axis=1)
    kr2 = jnp.concatenate([kr, kr], axis=1)
    aq, bq, ak, bk = aq_ref[...], bq_ref[...], ak_ref[...], bk_ref[...]
    ones = ones_ref[...]
    inv_qk = 1.0 / MLA_QK
    n_main = heads * HEAD_PAD
    pair = 2 * HEAD_PAD
    for hp in range(heads // 2):
        qa = qf[:, hp * pair:(hp + 1) * pair]
        ss = jnp.dot((qa * qa).astype(BF16), ones, preferred_element_type=F32)
        rq = lax.rsqrt(ss * inv_qk + EPS)
        ka = kf[:, hp * pair:(hp + 1) * pair] + kr2
        ss = jnp.dot((ka * ka).astype(BF16), ones, preferred_element_type=F32)
        rk = lax.rsqrt(ss * inv_qk + EPS)
        for e in range(2):
            h = 2 * hp + e
            lo, hi = e * HEAD_PAD, (e + 1) * HEAD_PAD
            qb = qf[:, n_main + h * HEAD_PAD:n_main + (h + 1) * HEAD_PAD]
            q_ref[0, h] = (rq[:, lo:hi] * (qa[:, lo:hi] * aq + qb * bq)).astype(BF16)
            k_ref[0, h] = (rk[:, lo:hi] * (ka[:, lo:hi] * ak + kr_sw * bk)).astype(BF16)
    tm = vf.shape[0]
    row = lax.broadcasted_iota(jnp.int32, (BF16_ROWS, tm), 0)
    extra = (row == 0).astype(F32)
    for p in range(heads // 2):
        vt = vf[:, p * LANES:(p + 1) * LANES].T
        v_ref[0, p] = jnp.concatenate([vt[:MLA_V], extra, vt[MLA_V:], extra], axis=0).astype(BF16)


def _inproj_call(x, sh, sc, g1, lw, tables, tm):
    B, T, D = x.shape
    pos = pl.BlockSpec((tm, HEAD_PAD), lambda b, i: (i, 0))
    heads = lw["heads"]
    d_lru, q_rank, kv_rank = lw["d_lru"], lw["q_rank"], lw["kv_rank"]
    d_pool = lw["w_in"].shape[1] - (2 * d_lru + q_rank + kv_rank + HEAD_PAD)
    tok = lambda w: pl.BlockSpec((1, tm, w), lambda b, i: (b, i, 0))
    vec = pl.BlockSpec((1, 1, D), lambda b, i: (b, 0, 0))
    hd = lambda n: pl.BlockSpec((1, n, tm, LANES), lambda b, i: (b, 0, i, 0))
    kern = functools.partial(_inproj_kernel, d_lru=d_lru, q_rank=q_rank, kv_rank=kv_rank, heads=heads)
    return pl.pallas_call(
        kern,
        grid=(B, T // tm),
        in_specs=[tok(D), vec, vec, _const_spec((1, D)), _const_spec(lw["w_in"].shape),
                  _const_spec((1, q_rank)), _const_spec(lw["w_uq"].shape),
                  _const_spec((1, kv_rank)), _const_spec(lw["w_k"].shape), _const_spec(lw["w_v"].shape),
                  _const_spec(lw["head_ones"].shape), pos, pos, pos, pos],
        out_specs=[tok(d_lru), tok(d_lru), tok(d_pool), hd(heads), hd(heads),
                   pl.BlockSpec((1, heads // 2, 2 * V_ROWS, tm), lambda b, i: (b, 0, 0, i))],
        out_shape=[jax.ShapeDtypeStruct((B, T, d_lru), F32),
                   jax.ShapeDtypeStruct((B, T, d_lru), F32),
                   jax.ShapeDtypeStruct((B, T, d_pool), F32),
                   jax.ShapeDtypeStruct((B, heads, T, LANES), BF16),
                   jax.ShapeDtypeStruct((B, heads, T, LANES), BF16),
                   jax.ShapeDtypeStruct((B, heads // 2, 2 * V_ROWS, T), BF16)],
        compiler_params=_cparams(2),
        name="in_proj",
    )(x, sh, sc, g1, lw["w_in"], lw["g_q_lat"], lw["w_uq"], lw["g_kv_lat"], lw["w_k"], lw["w_v"],
      lw["head_ones"], *tables)


def _roll_rows(a, shift):
    return pltpu.roll(a, shift % a.shape[0], axis=0)


def _scan8(a, b, reverse):
    n = a.shape[0]
    a3 = a.reshape(n // SUBLANES, SUBLANES, a.shape[1])
    b3 = b.reshape(n // SUBLANES, SUBLANES, b.shape[1])
    rid = lax.broadcasted_iota(jnp.int32, a3.shape, 1)
    for s in (1, 2, 4):
        if reverse:
            m = rid < SUBLANES - s
            sh = SUBLANES - s
        else:
            m = rid >= s
            sh = s
        a_sh = pltpu.roll(a3, sh, axis=1)
        b_sh = pltpu.roll(b3, sh, axis=1)
        b3 = jnp.where(m, a3 * b_sh + b3, b3)
        a3 = jnp.where(m, a3 * a_sh, a3)
    return a3, b3


def _seq_kernel(lx_ref, lg_ref, pu_ref, h0_ref, cw_ref, cb_ref, wg_ref, bg_ref, lam_ref, wp_ref,
                ps_ref, a_out_ref, p_out_ref, hfin_ref, xpad_ref, upad_ref, hf_ref, *, tc):
    T = lx_ref.shape[1]
    C = lx_ref.shape[2]
    nc = T // tc
    zeros_halo = jnp.zeros((HALO, C), F32)
    xpad_ref[0:HALO, :] = zeros_halo
    xpad_ref[HALO + T:HALO + T + HALO, :] = zeros_halo
    upad_ref[0:HALO, :] = zeros_halo
    upad_ref[HALO + T:HALO + T + HALO, :] = zeros_halo

    def fill(c, carry):
        r0 = pl.multiple_of(c * tc, tc)
        xpad_ref[pl.ds(HALO + r0, tc), :] = lx_ref[0, pl.ds(r0, tc), :]
        upad_ref[pl.ds(HALO + r0, tc), :] = pu_ref[0, pl.ds(r0, tc), :]
        return carry

    lax.fori_loop(0, nc, fill, 0)

    lam = lam_ref[...]
    coef = -LRU_C * (jnp.maximum(-lam, 0.0) + jnp.log(1.0 + jnp.exp(-jnp.abs(lam))))
    cw = cw_ref[...]
    cb = cb_ref[...]
    bg = bg_ref[...]

    def conv_chunk(r0):
        w = xpad_ref[pl.ds(r0 + HALO - SUBLANES, tc + 2 * SUBLANES), :]
        y = (cw[0:1] * _roll_rows(w, 1) + cw[1:2] * w + cw[2:3] * _roll_rows(w, -1)
             + cw[3:4] * _roll_rows(w, -2))
        return y[SUBLANES:SUBLANES + tc] + cb

    def coeffs(xc, d):
        g = jnp.dot(xc.astype(BF16), wg_ref[:, 2 * C * d:2 * C * (d + 1)],
                    preferred_element_type=F32) + bg[:, 2 * C * d:2 * C * (d + 1)]
        r = jax.nn.sigmoid(g[:, :C])
        i = jax.nn.sigmoid(g[:, C:])
        log_a = coef[d:d + 1] * r
        a = jnp.exp(log_a)
        b = jnp.sqrt(1.0 - a * a) * (i * xc)
        return a, b

    lane = lax.broadcasted_iota(jnp.int32, (1, C), 1)
    gdim = C // len(POOL_WINDOWS)

    def pool_chunk(r0):
        w = upad_ref[pl.ds(r0, tc + 2 * HALO), :]
        s2 = w + _roll_rows(w, 1)
        s4 = _roll_rows(s2, 1) + _roll_rows(s2, -1)
        s8 = _roll_rows(s4, 2) + _roll_rows(s4, -2)
        s16 = _roll_rows(s8, 4) + _roll_rows(s8, -4)
        t = r0 + lax.broadcasted_iota(jnp.int32, (tc, 1), 0)
        sel = None
        cnt = None
        for g, (win, s) in enumerate(zip(POOL_WINDOWS, (s2, s4, s8, s16))):
            sg = s[HALO:HALO + tc]
            cg = (jnp.minimum(t - win // 2 + win, T) - jnp.maximum(t - win // 2, 0)).astype(F32)
            if sel is None:
                sel, cnt = sg, jnp.broadcast_to(cg, (tc, C))
            else:
                m = lane >= g * gdim
                sel = jnp.where(m, sg, sel)
                cnt = jnp.where(m, cg, cnt)
        mixed = sel / cnt - w[HALO:HALO + tc]
        y = jnp.dot(mixed.astype(BF16), wp_ref[...], preferred_element_type=F32) * ps_ref[...]
        p_out_ref[0, pl.ds(r0, tc), :] = y.astype(p_out_ref.dtype)

    nv = tc // SUBLANES

    def fwd(c, h):
        r0 = pl.multiple_of(c * tc, tc)
        pool_chunk(r0)
        a, b = coeffs(conv_chunk(r0), 0)
        a3, b3 = _scan8(a, b, reverse=False)
        for j in range(nv):
            hj = b3[j] + a3[j] * h
            hf_ref[pl.ds(r0 + j * SUBLANES, SUBLANES), :] = hj
            h = jnp.broadcast_to(hj[SUBLANES - 1:SUBLANES], (SUBLANES, C))
        return h

    h = lax.fori_loop(0, nc, fwd, jnp.broadcast_to(h0_ref[0, 0:1, :], (SUBLANES, C)))
    hfin_ref[0, 0:1, :] = h[0:1]

    def bwd(cc, h):
        c = nc - 1 - cc
        r0 = pl.multiple_of(c * tc, tc)
        a, b = coeffs(conv_chunk(r0), 1)
        a3, b3 = _scan8(a, b, reverse=True)
        for j in range(nv - 1, -1, -1):
            hj = b3[j] + a3[j] * h
            rows = pl.ds(r0 + j * SUBLANES, SUBLANES)
            y = hj + hf_ref[rows, :]
            a_out_ref[0, rows, :] = (y * jax.nn.gelu(lg_ref[0, rows, :])).astype(a_out_ref.dtype)
            h = jnp.broadcast_to(hj[0:1], (SUBLANES, C))
        return h

    h = lax.fori_loop(0, nc, bwd, jnp.broadcast_to(h0_ref[0, 1:2, :], (SUBLANES, C)))
    hfin_ref[0, 1:2, :] = h[0:1]


def _seq_call(lx, lg, pu, h0, lw, tc):
    B, T, C = lx.shape
    seq = pl.BlockSpec((1, T, C), lambda b: (b, 0, 0))
    st = pl.BlockSpec((1, 2, C), lambda b: (b, 0, 0))
    return pl.pallas_call(
        functools.partial(_seq_kernel, tc=tc),
        grid=(B,),
        in_specs=[seq, seq, seq, st, _const_spec((CONV_W, C)), _const_spec((1, C)),
                  _const_spec((C, 4 * C)), _const_spec((1, 4 * C)), _const_spec((2, C)),
                  _const_spec((C, C)), _const_spec((1, C))],
        out_specs=[seq, seq, st],
        out_shape=[jax.ShapeDtypeStruct((B, T, C), BF16), jax.ShapeDtypeStruct((B, T, C), BF16),
                   jax.ShapeDtypeStruct((B, 2, C), F32)],
        scratch_shapes=[pltpu.VMEM((T + 2 * HALO, C), F32), pltpu.VMEM((T + 2 * HALO, C), F32),
                        pltpu.VMEM((T, C), F32)],
        compiler_params=_cparams(1),
        name="seq_mix",
    )(lx, lg, pu, h0, lw["conv_w"], lw["conv_b"], lw["w_gate"], lw["b_gate"], lw["lam"],
      lw["w_pool"], lw["pool_scale"])


def _attn_kernel(*refs, n_seg, tq_sub, kc):
    q_ref = refs[0]
    kv_refs = refs[1:1 + 2 * n_seg]
    o_ref = refs[1 + 2 * n_seg]
    s_scr = refs[2 + 2 * n_seg]
    n_sub = q_ref.shape[2] // tq_sub
    units = [(hh, j) for j in range(n_sub) for hh in range(2)]
    chunks = []
    off = 0
    for seg in range(n_seg):
        lk = kv_refs[2 * seg].shape[2]
        for st in range(0, lk, kc):
            sz = min(kc, lk - st)
            chunks.append((seg, st, sz, off))
            off += sz
    nch = len(chunks)
    mvec = {}
    col_max = {}
    acc = {}
    results = {}
    pv_hist = []
    depth = 5

    def score_chunk(u, c):
        hh, j = units[u]
        seg, st, sz, off = chunks[c]
        q = q_ref[0, hh, j * tq_sub:(j + 1) * tq_sub, :]
        k = kv_refs[2 * seg][0, hh, st:st + sz, :]
        s = lax.dot_general(k, q, (((1,), (1,)), ((), ())), preferred_element_type=F32)
        s_scr[u % 2, off:off + sz, :] = s
        cm = jnp.max(s.reshape(sz // SUBLANES, SUBLANES, tq_sub), axis=0)
        mvec[u] = cm if c == 0 else jnp.maximum(mvec[u], cm)
        if c == nch - 1:
            col_max[u] = jnp.max(mvec[u], axis=0, keepdims=True)

    def weight_chunk(u, c):
        hh, j = units[u]
        seg, st, sz, off = chunks[c]
        m = col_max[u]
        if len(pv_hist) >= depth:
            bits = pltpu.bitcast(pv_hist[-depth][MLA_V:MLA_V + 1], jnp.uint32)
            m = m + ((bits >> 16) >> 16).astype(F32)
        p = jnp.exp2((s_scr[u % 2, off:off + sz, :] - m).astype(BF16))
        vt = kv_refs[2 * seg + 1][0, 0, hh * V_ROWS:(hh + 1) * V_ROWS, st:st + sz]
        pv = jnp.dot(vt, p, preferred_element_type=F32)
        pv_hist.append(pv)
        acc[u] = pv if c == 0 else acc[u] + pv
        if c == nch - 1:
            results[units[u]] = acc[u][:MLA_V] / acc[u][MLA_V:MLA_V + 1]

    lead = nch + min(4, nch - 1)
    events = []
    for u in range(len(units)):
        for c in range(nch):
            events.append((u * nch + c - lead, 0, u, c))
            events.append((u * nch + c, 1, u, c))
    for _, kind, u, c in sorted(events):
        (weight_chunk if kind else score_chunk)(u, c)

    for j in range(n_sub):
        o_t = jnp.concatenate([results[(0, j)], results[(1, j)]], axis=0)
        o_ref[0, j * tq_sub:(j + 1) * tq_sub, :] = o_t.T.astype(o_ref.dtype)


def _attn_call(q, segs, tq):
    B, H, T, _ = q.shape
    tq_sub = min(tq, 256)
    in_specs = [pl.BlockSpec((1, 2, tq, LANES), lambda b, p, i: (b, p, i, 0))]
    args = [q]
    lk_total = 0
    for k, v in segs:
        lk = k.shape[2]
        lk_total += lk
        in_specs.append(pl.BlockSpec((1, 2, lk, LANES), lambda b, p, i: (b, p, 0, 0)))
        in_specs.append(pl.BlockSpec((1, 1, 2 * V_ROWS, lk), lambda b, p, i: (b, p, 0, 0)))
        args += [k, v]
    return pl.pallas_call(
        functools.partial(_attn_kernel, n_seg=len(segs), tq_sub=tq_sub, kc=256),
        grid=(B, H // 2, T // tq),
        in_specs=in_specs,
        out_specs=pl.BlockSpec((1, tq, LANES), lambda b, p, i: (b, i, p)),
        out_shape=jax.ShapeDtypeStruct((B, T, H * MLA_V), BF16),
        scratch_shapes=[pltpu.VMEM((2, lk_total, tq_sub), F32)],
        compiler_params=_cparams(3),
        name="attention",
    )(*args)


def _post_kernel(x_ref, a_ref, att_ref, p_ref, g1_ref, sh_ref, sc_ref, g2_ref, gn_ref, wo_ref,
                 w1_ref, w2_ref, o_ref, *, ff_chunk):
    d_lru = a_ref.shape[2]
    d_att = att_ref.shape[2]
    y = jnp.dot(a_ref[0], wo_ref[0:d_lru, :], preferred_element_type=F32)
    y += jnp.dot(att_ref[0], wo_ref[d_lru:d_lru + d_att, :], preferred_element_type=F32)
    y += jnp.dot(p_ref[0], wo_ref[d_lru + d_att:, :], preferred_element_type=F32)
    x1 = x_ref[0] + g1_ref[0] * y
    xn = x1 * lax.rsqrt(jnp.mean(x1 * x1, axis=-1, keepdims=True) + EPS)
    h = ((xn * gn_ref[...]) * (1.0 + sc_ref[0]) + sh_ref[0]).astype(BF16)
    d_ff = w1_ref.shape[1]
    acc = None
    for c in range(d_ff // ff_chunk):
        u = jnp.dot(h, w1_ref[:, c * ff_chunk:(c + 1) * ff_chunk], preferred_element_type=F32)
        u = jnp.square(jnp.maximum(u, 0.0)).astype(BF16)
        part = jnp.dot(u, w2_ref[c * ff_chunk:(c + 1) * ff_chunk, :], preferred_element_type=F32)
        acc = part if acc is None else acc + part
    o_ref[0] = x1 + g2_ref[0] * acc


def _post_call(x, a, att, p, g1, sh, sc, g2, gn, lw, tm):
    B, T, D = x.shape
    tok = lambda w: pl.BlockSpec((1, tm, w), lambda b, i: (b, i, 0))
    vec = pl.BlockSpec((1, 1, D), lambda b, i: (b, 0, 0))
    single = lambda shape: pl.BlockSpec(shape, lambda b, i: (0,) * len(shape),
                                        pipeline_mode=pl.Buffered(1))
    return pl.pallas_call(
        functools.partial(_post_kernel, ff_chunk=1024),
        grid=(B, T // tm),
        in_specs=[tok(D), tok(a.shape[2]), tok(att.shape[2]), tok(p.shape[2]), vec, vec, vec, vec,
                  _const_spec((1, D)), single(lw["w_out"].shape), single(lw["w_ff1"].shape),
                  single(lw["w_ff2"].shape)],
        out_specs=tok(D),
        out_shape=jax.ShapeDtypeStruct((B, T, D), F32),
        compiler_params=_cparams(2),
        name="out_proj_mlp",
    )(x, a, att, p, g1, sh, sc, g2, gn, lw["w_out"], lw["w_ff1"], lw["w_ff2"])


def _rope_partner():
    j = np.arange(MLA_ROPE)
    half = (j % (2 * ROPE_FREQS)) // ROPE_FREQS
    return np.where(half == 0, j + ROPE_FREQS, j - ROPE_FREQS)


def _block_diag(w):
    n, a, b = w.shape
    out = jnp.zeros((n * a, n * b), w.dtype)
    for i in range(n):
        out = out.at[i * a:(i + 1) * a, i * b:(i + 1) * b].set(w[i])
    return out


def _layer_weights(l, w_in, conv_w, conv_b, lru_w_a, lru_b_a, lru_w_x, lru_b_x, lru_lambda, g_q_lat,
                   w_uq, g_kv_lat, w_ukv, g_qn, g_kn, w_pool, pool_scale, w_out, w_ff1, w_ff2):
    d = w_in.shape[1]
    d_lru = conv_w.shape[2]
    q_rank = w_uq.shape[1]
    kv_rank = w_ukv.shape[1]
    heads = w_uq.shape[2] // MLA_QK
    partner = _rope_partner()
    o_kr = 2 * d_lru + q_rank + kv_rank
    wi = w_in[l]
    kr = wi[:, o_kr:o_kr + MLA_ROPE]
    krb = jnp.concatenate([jnp.zeros((d, MLA_NOPE), F32), kr, kr[:, partner]], axis=1)
    w_in_p = jnp.concatenate([wi[:, :o_kr], krb, wi[:, o_kr + MLA_ROPE:]], axis=1).astype(BF16)
    wq = w_uq[l].reshape(q_rank, heads, MLA_QK)
    pad = jnp.zeros((q_rank, heads, HEAD_PAD - MLA_QK), F32)
    wq_main = jnp.concatenate([wq, pad], axis=2)
    wq_partner = jnp.concatenate([jnp.zeros((q_rank, heads, MLA_NOPE), F32),
                                  wq[:, :, MLA_NOPE + partner], pad], axis=2)
    wq = jnp.concatenate([wq_main.reshape(q_rank, heads * HEAD_PAD),
                          wq_partner.reshape(q_rank, heads * HEAD_PAD)], axis=1)
    wkv = w_ukv[l].reshape(kv_rank, heads, MLA_NOPE + MLA_V)
    wk = jnp.concatenate([wkv[:, :, :MLA_NOPE], jnp.zeros((kv_rank, heads, HEAD_PAD - MLA_NOPE), F32)],
                         axis=2).reshape(kv_rank, heads * HEAD_PAD)
    wv = wkv[:, :, MLA_NOPE:].reshape(kv_rank, heads * MLA_V)
    zpad = jnp.zeros((HEAD_PAD - MLA_QK,), F32)
    znope = jnp.zeros((MLA_NOPE,), F32)
    q_scale = LOG2E / math.sqrt(MLA_QK)
    gq = jnp.concatenate([g_qn[l], zpad]) * q_scale
    gq_sw = jnp.concatenate([znope, g_qn[l][MLA_NOPE + partner], zpad]) * q_scale
    gk = jnp.concatenate([g_kn[l], zpad])
    gk_sw = jnp.concatenate([znope, g_kn[l][MLA_NOPE + partner], zpad])
    head_ones = _block_diag(jnp.ones((2, HEAD_PAD, HEAD_PAD), BF16))
    w_gate = jnp.concatenate([_block_diag(lru_w_a[l, 0]), _block_diag(lru_w_x[l, 0]),
                              _block_diag(lru_w_a[l, 1]), _block_diag(lru_w_x[l, 1])], axis=1)
    b_gate = jnp.concatenate([lru_b_a[l, 0], lru_b_x[l, 0], lru_b_a[l, 1], lru_b_x[l, 1]])
    return dict(
        heads=heads, d_lru=d_lru, q_rank=q_rank, kv_rank=kv_rank,
        w_in=w_in_p, g_q_lat=g_q_lat[l][None], w_uq=wq.astype(BF16), g_kv_lat=g_kv_lat[l][None],
        w_k=wk.astype(BF16), w_v=wv.astype(BF16), gq=gq[None], gq_sw=gq_sw[None], gk=gk[None],
        gk_sw=gk_sw[None], head_ones=head_ones,
        conv_w=conv_w[l], conv_b=conv_b[l][None], w_gate=w_gate.astype(BF16), b_gate=b_gate[None],
        lam=lru_lambda[l], w_pool=_block_diag(w_pool[l]).astype(BF16), pool_scale=pool_scale[l][None],
        w_out=w_out[l].astype(BF16), w_ff1=w_ff1[l].astype(BF16), w_ff2=w_ff2[l].astype(BF16))


def _rope_tables(T):
    partner = _rope_partner()
    j = np.arange(MLA_ROPE)
    axis = j // (2 * ROPE_FREQS)
    half = (j % (2 * ROPE_FREQS)) // ROPE_FREQS
    f = j % ROPE_FREQS
    del partner
    t = jnp.arange(T)
    pos = jnp.stack([(t // GRID_W).astype(F32), (t % GRID_W).astype(F32)], axis=-1)
    freqs = jnp.power(ROPE_BASE, -jnp.arange(ROPE_FREQS, dtype=F32) / ROPE_FREQS)
    ang = pos[:, axis] * freqs[f]
    sign = jnp.asarray(np.where(half == 0, -1.0, 1.0), F32)
    ct = jnp.concatenate([jnp.ones((T, MLA_NOPE), F32), jnp.cos(ang),
                          jnp.zeros((T, HEAD_PAD - MLA_QK), F32)], axis=1)
    st = jnp.concatenate([jnp.zeros((T, MLA_NOPE), F32), jnp.sin(ang) * sign,
                          jnp.zeros((T, HEAD_PAD - MLA_QK), F32)], axis=1)
    return ct, st


def _no_rope_tables(T):
    lane = np.arange(HEAD_PAD)
    ct = jnp.broadcast_to(jnp.asarray((lane < MLA_QK).astype(np.float32)), (T, HEAD_PAD))
    return ct, jnp.zeros((T, HEAD_PAD), F32)


def _tile(n, pref):
    t = min(n, pref)
    while n % t:
        t //= 2
    return t


def kernel(x, c, ctx, c_ctx, w_mod, b_mod, g_norm1, g_norm2, w_in, conv_w, conv_b, lru_w_a, lru_b_a,
           lru_w_x, lru_b_x, lru_lambda, g_q_lat, w_uq, g_kv_lat, w_ukv, g_qn, g_kn, w_pool, pool_scale,
           w_out, w_ff1, w_ff2):
    B, L, D = x.shape
    Lc = ctx.shape[1]
    depth = w_mod.shape[0]
    d_lru = conv_w.shape[2]

    rows = -(-(B + 1) // SUBLANES) * SUBLANES
    cvec = jnp.zeros((rows, D), F32).at[:B].set(c).at[B].set(c_ctx)
    mods = _mod_call(cvec, w_mod, b_mod)

    rope_l = _rope_tables(L)
    rope_c = _no_rope_tables(Lc)
    tm_l, tm_c = _tile(L, 256), _tile(Lc, 256)
    tp_l, tp_c = _tile(L, 512), _tile(Lc, 512)
    tq_l, tq_c = _tile(L, 1024), _tile(Lc, 1024)
    tc_l, tc_c = _tile(L, 256), _tile(Lc, 256)

    h = ctx
    zeros_state = jnp.zeros((B, 2, d_lru), F32)
    for l in range(depth):
        last = l == depth - 1
        lw = _layer_weights(l, w_in, conv_w, conv_b, lru_w_a, lru_b_a, lru_w_x, lru_b_x, lru_lambda,
                            g_q_lat, w_uq, g_kv_lat, w_ukv, g_qn, g_kn, w_pool, pool_scale, w_out,
                            w_ff1, w_ff2)
        ml = mods[l, :B].reshape(B, 1, 6, D)
        mc = jnp.broadcast_to(mods[l, B].reshape(1, 1, 6, D), (B, 1, 6, D))
        sh1, sc1, g1, sh2, sc2, g2 = (ml[:, :, i] for i in range(6))
        csh1, csc1, cg1, csh2, csc2, cg2 = (mc[:, :, i] for i in range(6))
        gn1 = g_norm1[l][None]
        gn2 = g_norm2[l][None]

        tab_c, tab_l = (
            (ct * lw["gq"], st * lw["gq_sw"], ct * lw["gk"], st * lw["gk_sw"]) for ct, st in (rope_c, rope_l))
        lx_c, lg_c, pu_c, q_c, k_c, v_c = _inproj_call(h, csh1, csc1, gn1, lw, tab_c, tm_c)
        a_c, p_c, hfin = _seq_call(lx_c, lg_c, pu_c, zeros_state, lw, tc_c)
        lx_l, lg_l, pu_l, q_l, k_l, v_l = _inproj_call(x, sh1, sc1, gn1, lw, tab_l, tm_l)
        a_l, p_l, _ = _seq_call(lx_l, lg_l, pu_l, hfin, lw, tc_l)
        att_l = _attn_call(q_l, [(k_c, v_c), (k_l, v_l)], tq_l)
        x = _post_call(x, a_l, att_l, p_l, g1, sh2, sc2, g2, gn2, lw, tp_l)
        if not last:
            att_c = _attn_call(q_c, [(k_c, v_c)], tq_c)
            h = _post_call(h, a_c, att_c, p_c, cg1, csh2, csc2, cg2, gn2, lw, tp_c)
    return x
```

```python
import functools
import math

import jax
import jax.numpy as jnp
import numpy as np
from jax import lax
from jax.experimental import pallas as pl
from jax.experimental.pallas import tpu as pltpu

F32 = jnp.float32
BF16 = jnp.bfloat16

EPS = 1e-6
GRID_W = 64
LRU_HEADS = 4
LRU_C = 8.0
CONV_W = 4
MLA_V = 64
MLA_NOPE = 64
MLA_ROPE = 32
MLA_QK = MLA_NOPE + MLA_ROPE
ROPE_FREQS = MLA_ROPE // 4
ROPE_BASE = 10000.0
POOL_WINDOWS = (2, 4, 8, 16)

LANES = 128
SUBLANES = 8
HEAD_PAD = LANES
BF16_ROWS = 16
V_ROWS = MLA_V + BF16_ROWS
HALO = 16
VMEM_LIMIT = 56 * 1024 * 1024

LOG2E = 1.4426950408889634


def _cparams(n_grid, flags=None):
    return pltpu.CompilerParams(dimension_semantics=("arbitrary",) * n_grid,
                                vmem_limit_bytes=VMEM_LIMIT, flags=flags)


def _const_spec(shape):
    zeros = (0,) * len(shape)
    return pl.BlockSpec(shape, lambda *_: zeros)


def _mod_kernel(c_ref, w_ref, b_ref, o_ref):
    c = c_ref[...]
    act = (c * jax.nn.sigmoid(c)).astype(BF16)
    o_ref[0] = jnp.dot(act, w_ref[0].astype(BF16), preferred_element_type=F32) + b_ref[0]


def _mod_call(cvec, w_mod, b_mod):
    depth, d, n = w_mod.shape
    rows = cvec.shape[0]
    tn = 1024
    return pl.pallas_call(
        _mod_kernel,
        grid=(depth, n // tn),
        in_specs=[pl.BlockSpec((rows, d), lambda l, j: (0, 0)),
                  pl.BlockSpec((1, d, tn), lambda l, j: (l, 0, j)),
                  pl.BlockSpec((1, 1, tn), lambda l, j: (l, 0, j))],
        out_specs=pl.BlockSpec((1, rows, tn), lambda l, j: (l, 0, j)),
        out_shape=jax.ShapeDtypeStruct((depth, rows, n), F32),
        compiler_params=_cparams(2),
        name="adaln_mod",
    )(cvec, w_mod, b_mod.reshape(depth, 1, n))


def _inproj_kernel(x_ref, sh_ref, sc_ref, g1_ref, win_ref, gql_ref, wuq_ref, gkvl_ref, wk_ref,
                   wv_ref, ones_ref, aq_ref, bq_ref, ak_ref, bk_ref,
                   lx_ref, lg_ref, pu_ref, q_ref, k_ref, v_ref, *, d_lru, q_rank, kv_rank, heads):
    x = x_ref[0]
    xn = x * lax.rsqrt(jnp.mean(x * x, axis=-1, keepdims=True) + EPS)
    hx = (xn * g1_ref[...]) * (1.0 + sc_ref[0]) + sh_ref[0]
    z = jnp.dot(hx.astype(BF16), win_ref[...], preferred_element_type=F32)
    o = 0
    lx_ref[0] = z[:, o:o + d_lru]; o += d_lru
    lg_ref[0] = z[:, o:o + d_lru]; o += d_lru
    ql = z[:, o:o + q_rank]; o += q_rank
    kvl = z[:, o:o + kv_rank]; o += kv_rank
    krb = z[:, o:o + HEAD_PAD]; o += HEAD_PAD
    pu_ref[0] = z[:, o:]

    qln = ql * lax.rsqrt(jnp.mean(ql * ql, axis=-1, keepdims=True) + EPS) * gql_ref[...]
    qf = jnp.dot(qln.astype(BF16), wuq_ref[...], preferred_element_type=F32)
    kvn = kvl * lax.rsqrt(jnp.mean(kvl * kvl, axis=-1, keepdims=True) + EPS) * gkvl_ref[...]
    kvn = kvn.astype(BF16)
    kf = jnp.dot(kvn, wk_ref[...], preferred_element_type=F32)
    vf = jnp.dot(kvn, wv_ref[...], preferred_element_type=F32)

    lane = lax.broadcasted_iota(jnp.int32, (1, HEAD_PAD), 1)
    kr = jnp.where((lane >= MLA_NOPE) & (lane < MLA_QK), krb, 0.0)
    kr_sw = pltpu.roll(krb, HEAD_PAD - MLA_ROPE, axis=1)
    kr2 = jnp.concatenate([kr, kr], axis=1)
    aq, bq, ak, bk = aq_ref[...], bq_ref[...], ak_ref[...], bk_ref[...]
    ones = ones_ref[...]
    inv_qk = 1.0 / MLA_QK
    n_main = heads * HEAD_PAD
    pair = 2 * HEAD_PAD
    for hp in range(heads // 2):
        qa = qf[:, hp * pair:(hp + 1) * pair]
        ss = jnp.dot((qa * qa).astype(BF16), ones, preferred_element_type=F32)
        rq = lax.rsqrt(ss * inv_qk + EPS)
        ka = kf[:, hp * pair:(hp + 1) * pair] + kr2
        ss = jnp.dot((ka * ka).astype(BF16), ones, preferred_element_type=F32)
        rk = lax.rsqrt(ss * inv_qk + EPS)
        for e in range(2):
            h = 2 * hp + e
            lo, hi = e * HEAD_PAD, (e + 1) * HEAD_PAD
            qb = qf[:, n_main + h * HEAD_PAD:n_main + (h + 1) * HEAD_PAD]
            q_ref[0, h] = (rq[:, lo:hi] * (qa[:, lo:hi] * aq + qb * bq)).astype(BF16)
            k_ref[0, h] = (rk[:, lo:hi] * (ka[:, lo:hi] * ak + kr_sw * bk)).astype(BF16)
    tm = vf.shape[0]
    row = lax.broadcasted_iota(jnp.int32, (BF16_ROWS, tm), 0)
    extra = (row == 0).astype(F32)
    for p in range(heads // 2):
        vt = vf[:, p * LANES:(p + 1) * LANES].T
        v_ref[0, p] = jnp.concatenate([vt[:MLA_V], extra, vt[MLA_V:], extra], axis=0).astype(BF16)


def _inproj_call(x, sh, sc, g1, lw, tables, tm):
    B, T, D = x.shape
    pos = pl.BlockSpec((tm, HEAD_PAD), lambda b, i: (i, 0))
    heads = lw["heads"]
    d_lru, q_rank, kv_rank = lw["d_lru"], lw["q_rank"], lw["kv_rank"]
    d_pool = lw["w_in"].shape[1] - (2 * d_lru + q_rank + kv_rank + HEAD_PAD)
    tok = lambda w: pl.BlockSpec((1, tm, w), lambda b, i: (b, i, 0))
    vec = pl.BlockSpec((1, 1, D), lambda b, i: (b, 0, 0))
    hd = lambda n: pl.BlockSpec((1, n, tm, LANES), lambda b, i: (b, 0, i, 0))
    kern = functools.partial(_inproj_kernel, d_lru=d_lru, q_rank=q_rank, kv_rank=kv_rank, heads=heads)
    return pl.pallas_call(
        kern,
        grid=(B, T // tm),
        in_specs=[tok(D), vec, vec, _const_spec((1, D)), _const_spec(lw["w_in"].shape),
                  _const_spec((1, q_rank)), _const_spec(lw["w_uq"].shape),
                  _const_spec((1, kv_rank)), _const_spec(lw["w_k"].shape), _const_spec(lw["w_v"].shape),
                  _const_spec(lw["head_ones"].shape), pos, pos, pos, pos],
        out_specs=[tok(d_lru), tok(d_lru), tok(d_pool), hd(heads), hd(heads),
                   pl.BlockSpec((1, heads // 2, 2 * V_ROWS, tm), lambda b, i: (b, 0, 0, i))],
        out_shape=[jax.ShapeDtypeStruct((B, T, d_lru), F32),
                   jax.ShapeDtypeStruct((B, T, d_lru), F32),
                   jax.ShapeDtypeStruct((B, T, d_pool), F32),
                   jax.ShapeDtypeStruct((B, heads, T, LANES), BF16),
                   jax.ShapeDtypeStruct((B, heads, T, LANES), BF16),
                   jax.ShapeDtypeStruct((B, heads // 2, 2 * V_ROWS, T), BF16)],
        compiler_params=_cparams(2),
        name="in_proj",
    )(x, sh, sc, g1, lw["w_in"], lw["g_q_lat"], lw["w_uq"], lw["g_kv_lat"], lw["w_k"], lw["w_v"],
      lw["head_ones"], *tables)


def _roll_rows(a, shift):
    return pltpu.roll(a, shift % a.shape[0], axis=0)


def _scan8(a, b, reverse):
    n = a.shape[0]
    a3 = a.reshape(n // SUBLANES, SUBLANES, a.shape[1])
    b3 = b.reshape(n // SUBLANES, SUBLANES, b.shape[1])
    rid = lax.broadcasted_iota(jnp.int32, a3.shape, 1)
    for s in (1, 2, 4):
        if reverse:
            m = rid < SUBLANES - s
            sh = SUBLANES - s
        else:
            m = rid >= s
            sh = s
        a_sh = pltpu.roll(a3, sh, axis=1)
        b_sh = pltpu.roll(b3, sh, axis=1)
        b3 = jnp.where(m, a3 * b_sh + b3, b3)
        a3 = jnp.where(m, a3 * a_sh, a3)
    return a3, b3


def _seq_kernel(lx_ref, lg_ref, pu_ref, h0_ref, cw_ref, cb_ref, wg_ref, bg_ref, lam_ref, wp_ref,
                ps_ref, a_out_ref, p_out_ref, hfin_ref, xpad_ref, upad_ref, hf_ref, xc_ref, *, tc):
    T = lx_ref.shape[1]
    C = lx_ref.shape[2]
    nc = T // tc
    zeros_halo = jnp.zeros((HALO, C), F32)
    xpad_ref[0:HALO, :] = zeros_halo
    xpad_ref[HALO + T:HALO + T + HALO, :] = zeros_halo
    upad_ref[0:HALO, :] = zeros_halo
    upad_ref[HALO + T:HALO + T + HALO, :] = zeros_halo

    def fill(c, carry):
        r0 = pl.multiple_of(c * tc, tc)
        xpad_ref[pl.ds(HALO + r0, tc), :] = lx_ref[0, pl.ds(r0, tc), :]
        upad_ref[pl.ds(HALO + r0, tc), :] = pu_ref[0, pl.ds(r0, tc), :]
        return carry

    lax.fori_loop(0, nc, fill, 0)

    lam = lam_ref[...]
    coef = -LRU_C * (jnp.maximum(-lam, 0.0) + jnp.log(1.0 + jnp.exp(-jnp.abs(lam))))
    cw = cw_ref[...]
    cb = cb_ref[...]
    bg = bg_ref[...]

    def conv_chunk(r0):
        w = xpad_ref[pl.ds(r0 + HALO - SUBLANES, tc + 2 * SUBLANES), :]
        y = (cw[0:1] * _roll_rows(w, 1) + cw[1:2] * w + cw[2:3] * _roll_rows(w, -1)
             + cw[3:4] * _roll_rows(w, -2))
        return y[SUBLANES:SUBLANES + tc] + cb

    def coeffs(xc, d):
        g = jnp.dot(xc.astype(BF16), wg_ref[:, 2 * C * d:2 * C * (d + 1)],
                    preferred_element_type=F32) + bg[:, 2 * C * d:2 * C * (d + 1)]
        r = jax.nn.sigmoid(g[:, :C])
        i = jax.nn.sigmoid(g[:, C:])
        log_a = coef[d:d + 1] * r
        a = jnp.exp(log_a)
        b = jnp.exp2(0.5 * jnp.log2(1.0 - a * a)) * (i * xc)
        return a, b

    lane = lax.broadcasted_iota(jnp.int32, (1, C), 1)
    gdim = C // len(POOL_WINDOWS)

    inv_win = None
    for g, win in enumerate(POOL_WINDOWS):
        inv_g = jnp.full((1, C), 1.0 / win, F32)
        inv_win = inv_g if inv_win is None else jnp.where(lane >= g * gdim, inv_g, inv_win)

    def pool_rows(r0, n, edge):
        w = upad_ref[pl.ds(r0, n + 2 * HALO), :]
        s2 = w + _roll_rows(w, 1)
        s4 = _roll_rows(s2, 1) + _roll_rows(s2, -1)
        s8 = _roll_rows(s4, 2) + _roll_rows(s4, -2)
        s16 = _roll_rows(s8, 4) + _roll_rows(s8, -4)
        sel = s2[HALO:HALO + n]
        for g, s in enumerate((s4, s8, s16), start=1):
            sel = jnp.where(lane >= g * gdim, s[HALO:HALO + n], sel)
        if edge:
            t = r0 + lax.broadcasted_iota(jnp.int32, (n, 1), 0)
            cnt = None
            for g, win in enumerate(POOL_WINDOWS):
                cg = (jnp.minimum(t - win // 2 + win, T) - jnp.maximum(t - win // 2, 0)).astype(F32)
                cg = jnp.broadcast_to(cg, (n, C))
                cnt = cg if cnt is None else jnp.where(lane >= g * gdim, cg, cnt)
            mean = sel / cnt
        else:
            mean = sel * inv_win
        mixed = mean - w[HALO:HALO + n]
        y = jnp.dot(mixed.astype(BF16), wp_ref[...], preferred_element_type=F32) * ps_ref[...]
        p_out_ref[0, pl.ds(r0, n), :] = y.astype(p_out_ref.dtype)

    nv = tc // SUBLANES

    def fwd(c, h):
        r0 = pl.multiple_of(c * tc, tc)
        pool_rows(r0, tc, False)
        xc = conv_chunk(r0)
        xc_ref[pl.ds(r0, tc), :] = xc
        a, b = coeffs(xc, 0)
        a3, b3 = _scan8(a, b, reverse=False)
        for j in range(nv):
            hj = b3[j] + a3[j] * h
            hf_ref[pl.ds(r0 + j * SUBLANES, SUBLANES), :] = hj
            h = jnp.broadcast_to(hj[SUBLANES - 1:SUBLANES], (SUBLANES, C))
        return h

    h = lax.fori_loop(0, nc, fwd, jnp.broadcast_to(h0_ref[0, 0:1, :], (SUBLANES, C)))
    hfin_ref[0, 0:1, :] = h[0:1]

    def bwd(cc, h):
        c = nc - 1 - cc
        r0 = pl.multiple_of(c * tc, tc)
        a, b = coeffs(xc_ref[pl.ds(r0, tc), :], 1)
        a3, b3 = _scan8(a, b, reverse=True)
        for j in range(nv - 1, -1, -1):
            hj = b3[j] + a3[j] * h
            rows = pl.ds(r0 + j * SUBLANES, SUBLANES)
            y = hj + hf_ref[rows, :]
            a_out_ref[0, rows, :] = (y * jax.nn.gelu(lg_ref[0, rows, :])).astype(a_out_ref.dtype)
            h = jnp.broadcast_to(hj[0:1], (SUBLANES, C))
        return h

    h = lax.fori_loop(0, nc, bwd, jnp.broadcast_to(h0_ref[0, 1:2, :], (SUBLANES, C)),
                      unroll=2 if nc % 2 == 0 else 1)
    hfin_ref[0, 1:2, :] = h[0:1]

    pool_rows(0, BF16_ROWS, True)
    pool_rows(T - BF16_ROWS, BF16_ROWS, True)


def _seq_call(lx, lg, pu, h0, lw, tc):
    B, T, C = lx.shape
    seq = pl.BlockSpec((1, T, C), lambda b: (b, 0, 0))
    st = pl.BlockSpec((1, 2, C), lambda b: (b, 0, 0))
    return pl.pallas_call(
        functools.partial(_seq_kernel, tc=tc),
        grid=(B,),
        in_specs=[seq, seq, seq, st, _const_spec((CONV_W, C)), _const_spec((1, C)),
                  _const_spec((C, 4 * C)), _const_spec((1, 4 * C)), _const_spec((2, C)),
                  _const_spec((C, C)), _const_spec((1, C))],
        out_specs=[seq, seq, st],
        out_shape=[jax.ShapeDtypeStruct((B, T, C), BF16), jax.ShapeDtypeStruct((B, T, C), BF16),
                   jax.ShapeDtypeStruct((B, 2, C), F32)],
        scratch_shapes=[pltpu.VMEM((T + 2 * HALO, C), F32), pltpu.VMEM((T + 2 * HALO, C), F32),
                        pltpu.VMEM((T, C), F32), pltpu.VMEM((T, C), F32)],
        compiler_params=_cparams(1),
        name="seq_mix",
    )(lx, lg, pu, h0, lw["conv_w"], lw["conv_b"], lw["w_gate"], lw["b_gate"], lw["lam"],
      lw["w_pool"], lw["pool_scale"])


def _attn_kernel(*refs, n_seg, tq_sub, kc):
    q_ref = refs[0]
    kv_refs = refs[1:1 + 2 * n_seg]
    o_ref = refs[1 + 2 * n_seg]
    s_scr = refs[2 + 2 * n_seg]
    n_sub = q_ref.shape[2] // tq_sub
    units = [(hh, j) for j in range(n_sub) for hh in range(2)]
    chunks = []
    off = 0
    for seg in range(n_seg):
        lk = kv_refs[2 * seg].shape[2]
        for st in range(0, lk, kc):
            sz = min(kc, lk - st)
            chunks.append((seg, st, sz, off))
            off += sz
    nch = len(chunks)
    mvec = {}
    col_max = {}
    acc = {}
    results = {}
    pv_hist = []
    depth = 5

    def score_chunk(u, c):
        hh, j = units[u]
        seg, st, sz, off = chunks[c]
        q = q_ref[0, hh, j * tq_sub:(j + 1) * tq_sub, :]
        k = kv_refs[2 * seg][0, hh, st:st + sz, :]
        s = lax.dot_general(k, q, (((1,), (1,)), ((), ())), preferred_element_type=F32)
        s_scr[u % 2, off:off + sz, :] = s
        cm = jnp.max(s.reshape(sz // SUBLANES, SUBLANES, tq_sub), axis=0)
        mvec[u] = cm if c == 0 else jnp.maximum(mvec[u], cm)
        if c == nch - 1:
            col_max[u] = jnp.max(mvec[u], axis=0, keepdims=True)

    def weight_chunk(u, c):
        hh, j = units[u]
        seg, st, sz, off = chunks[c]
        m = col_max[u]
        if len(pv_hist) >= depth:
            bits = pltpu.bitcast(pv_hist[-depth][MLA_V:MLA_V + 1], jnp.uint32)
            m = m + ((bits >> 16) >> 16).astype(F32)
        p = jnp.exp2((s_scr[u % 2, off:off + sz, :] - m).astype(BF16))
        vt = kv_refs[2 * seg + 1][0, 0, hh * V_ROWS:(hh + 1) * V_ROWS, st:st + sz]
        pv = jnp.dot(vt, p, preferred_element_type=F32)
        pv_hist.append(pv)
        acc[u] = pv if c == 0 else acc[u] + pv
        if c == nch - 1:
            results[units[u]] = acc[u][:MLA_V] / acc[u][MLA_V:MLA_V + 1]

    lead = nch + min(4, nch - 1)
    events = []
    for u in range(len(units)):
        for c in range(nch):
            events.append((u * nch + c - lead, 0, u, c))
            events.append((u * nch + c, 1, u, c))
    for _, kind, u, c in sorted(events):
        (weight_chunk if kind else score_chunk)(u, c)

    for j in range(n_sub):
        o_t = jnp.concatenate([results[(0, j)], results[(1, j)]], axis=0)
        o_ref[0, j * tq_sub:(j + 1) * tq_sub, :] = o_t.T.astype(o_ref.dtype)


def _attn_call(q, segs, tq):
    B, H, T, _ = q.shape
    tq_sub = min(tq, 256)
    in_specs = [pl.BlockSpec((1, 2, tq, LANES), lambda b, p, i: (b, p, i, 0))]
    args = [q]
    lk_total = 0
    for k, v in segs:
        lk = k.shape[2]
        lk_total += lk
        in_specs.append(pl.BlockSpec((1, 2, lk, LANES), lambda b, p, i: (b, p, 0, 0)))
        in_specs.append(pl.BlockSpec((1, 1, 2 * V_ROWS, lk), lambda b, p, i: (b, p, 0, 0)))
        args += [k, v]
    return pl.pallas_call(
        functools.partial(_attn_kernel, n_seg=len(segs), tq_sub=tq_sub, kc=256),
        grid=(B, H // 2, T // tq),
        in_specs=in_specs,
        out_specs=pl.BlockSpec((1, tq, LANES), lambda b, p, i: (b, i, p)),
        out_shape=jax.ShapeDtypeStruct((B, T, H * MLA_V), BF16),
        scratch_shapes=[pltpu.VMEM((2, lk_total, tq_sub), F32)],
        compiler_params=_cparams(3),
        name="attention",
    )(*args)


def _post_kernel(x_ref, a_ref, att_ref, p_ref, g1_ref, sh_ref, sc_ref, g2_ref, gn_ref, wo_ref,
                 w1_ref, w2_ref, o_ref, *, ff_chunk):
    d_lru = a_ref.shape[2]
    d_att = att_ref.shape[2]
    y = jnp.dot(a_ref[0], wo_ref[0:d_lru, :], preferred_element_type=F32)
    y += jnp.dot(att_ref[0], wo_ref[d_lru:d_lru + d_att, :], preferred_element_type=F32)
    y += jnp.dot(p_ref[0], wo_ref[d_lru + d_att:, :], preferred_element_type=F32)
    x1 = x_ref[0] + g1_ref[0] * y
    xn = x1 * lax.rsqrt(jnp.mean(x1 * x1, axis=-1, keepdims=True) + EPS)
    h = ((xn * gn_ref[...]) * (1.0 + sc_ref[0]) + sh_ref[0]).astype(BF16)
    d_ff = w1_ref.shape[1]
    acc = None
    for c in range(d_ff // ff_chunk):
        u = jnp.dot(h, w1_ref[:, c * ff_chunk:(c + 1) * ff_chunk], preferred_element_type=F32)
        u = jnp.square(jnp.maximum(u, 0.0)).astype(BF16)
        part = jnp.dot(u, w2_ref[c * ff_chunk:(c + 1) * ff_chunk, :], preferred_element_type=F32)
        acc = part if acc is None else acc + part
    o_ref[0] = x1 + g2_ref[0] * acc


def _post_call(x, a, att, p, g1, sh, sc, g2, gn, lw, tm):
    B, T, D = x.shape
    tok = lambda w: pl.BlockSpec((1, tm, w), lambda b, i: (b, i, 0))
    vec = pl.BlockSpec((1, 1, D), lambda b, i: (b, 0, 0))
    single = lambda shape: pl.BlockSpec(shape, lambda b, i: (0,) * len(shape),
                                        pipeline_mode=pl.Buffered(1))
    return pl.pallas_call(
        functools.partial(_post_kernel, ff_chunk=1024),
        grid=(B, T // tm),
        in_specs=[tok(D), tok(a.shape[2]), tok(att.shape[2]), tok(p.shape[2]), vec, vec, vec, vec,
                  _const_spec((1, D)), single(lw["w_out"].shape), single(lw["w_ff1"].shape),
                  single(lw["w_ff2"].shape)],
        out_specs=tok(D),
        out_shape=jax.ShapeDtypeStruct((B, T, D), F32),
        compiler_params=_cparams(2),
        name="out_proj_mlp",
    )(x, a, att, p, g1, sh, sc, g2, gn, lw["w_out"], lw["w_ff1"], lw["w_ff2"])


def _rope_partner():
    j = np.arange(MLA_ROPE)
    half = (j % (2 * ROPE_FREQS)) // ROPE_FREQS
    return np.where(half == 0, j + ROPE_FREQS, j - ROPE_FREQS)


def _block_diag(w):
    n, a, b = w.shape
    out = jnp.zeros((n * a, n * b), w.dtype)
    for i in range(n):
        out = out.at[i * a:(i + 1) * a, i * b:(i + 1) * b].set(w[i])
    return out


def _layer_weights(l, w_in, conv_w, conv_b, lru_w_a, lru_b_a, lru_w_x, lru_b_x, lru_lambda, g_q_lat,
                   w_uq, g_kv_lat, w_ukv, g_qn, g_kn, w_pool, pool_scale, w_out, w_ff1, w_ff2):
    d = w_in.shape[1]
    d_lru = conv_w.shape[2]
    q_rank = w_uq.shape[1]
    kv_rank = w_ukv.shape[1]
    heads = w_uq.shape[2] // MLA_QK
    partner = _rope_partner()
    o_kr = 2 * d_lru + q_rank + kv_rank
    wi = w_in[l]
    kr = wi[:, o_kr:o_kr + MLA_ROPE]
    krb = jnp.concatenate([jnp.zeros((d, MLA_NOPE), F32), kr, kr[:, partner]], axis=1)
    w_in_p = jnp.concatenate([wi[:, :o_kr], krb, wi[:, o_kr + MLA_ROPE:]], axis=1).astype(BF16)
    wq = w_uq[l].reshape(q_rank, heads, MLA_QK)
    pad = jnp.zeros((q_rank, heads, HEAD_PAD - MLA_QK), F32)
    wq_main = jnp.concatenate([wq, pad], axis=2)
    wq_partner = jnp.concatenate([jnp.zeros((q_rank, heads, MLA_NOPE), F32),
                                  wq[:, :, MLA_NOPE + partner], pad], axis=2)
    wq = jnp.concatenate([wq_main.reshape(q_rank, heads * HEAD_PAD),
                          wq_partner.reshape(q_rank, heads * HEAD_PAD)], axis=1)
    wkv = w_ukv[l].reshape(kv_rank, heads, MLA_NOPE + MLA_V)
    wk = jnp.concatenate([wkv[:, :, :MLA_NOPE], jnp.zeros((kv_rank, heads, HEAD_PAD - MLA_NOPE), F32)],
                         axis=2).reshape(kv_rank, heads * HEAD_PAD)
    wv = wkv[:, :, MLA_NOPE:].reshape(kv_rank, heads * MLA_V)
    zpad = jnp.zeros((HEAD_PAD - MLA_QK,), F32)
    znope = jnp.zeros((MLA_NOPE,), F32)
    q_scale = LOG2E / math.sqrt(MLA_QK)
    gq = jnp.concatenate([g_qn[l], zpad]) * q_scale
    gq_sw = jnp.concatenate([znope, g_qn[l][MLA_NOPE + partner], zpad]) * q_scale
    gk = jnp.concatenate([g_kn[l], zpad])
    gk_sw = jnp.concatenate([znope, g_kn[l][MLA_NOPE + partner], zpad])
    head_ones = _block_diag(jnp.ones((2, HEAD_PAD, HEAD_PAD), BF16))
    w_gate = jnp.concatenate([_block_diag(lru_w_a[l, 0]), _block_diag(lru_w_x[l, 0]),
                              _block_diag(lru_w_a[l, 1]), _block_diag(lru_w_x[l, 1])], axis=1)
    b_gate = jnp.concatenate([lru_b_a[l, 0], lru_b_x[l, 0], lru_b_a[l, 1], lru_b_x[l, 1]])
    return dict(
        heads=heads, d_lru=d_lru, q_rank=q_rank, kv_rank=kv_rank,
        w_in=w_in_p, g_q_lat=g_q_lat[l][None], w_uq=wq.astype(BF16), g_kv_lat=g_kv_lat[l][None],
        w_k=wk.astype(BF16), w_v=wv.astype(BF16), gq=gq[None], gq_sw=gq_sw[None], gk=gk[None],
        gk_sw=gk_sw[None], head_ones=head_ones,
        conv_w=conv_w[l], conv_b=conv_b[l][None], w_gate=w_gate.astype(BF16), b_gate=b_gate[None],
        lam=lru_lambda[l], w_pool=_block_diag(w_pool[l]).astype(BF16), pool_scale=pool_scale[l][None],
        w_out=w_out[l].astype(BF16), w_ff1=w_ff1[l].astype(BF16), w_ff2=w_ff2[l].astype(BF16))


def _rope_tables(T):
    partner = _rope_partner()
    j = np.arange(MLA_ROPE)
    axis = j // (2 * ROPE_FREQS)
    half = (j % (2 * ROPE_FREQS)) // ROPE_FREQS
    f = j % ROPE_FREQS
    del partner
    t = jnp.arange(T)
    pos = jnp.stack([(t // GRID_W).astype(F32), (t % GRID_W).astype(F32)], axis=-1)
    freqs = jnp.power(ROPE_BASE, -jnp.arange(ROPE_FREQS, dtype=F32) / ROPE_FREQS)
    ang = pos[:, axis] * freqs[f]
    sign = jnp.asarray(np.where(half == 0, -1.0, 1.0), F32)
    ct = jnp.concatenate([jnp.ones((T, MLA_NOPE), F32), jnp.cos(ang),
                          jnp.zeros((T, HEAD_PAD - MLA_QK), F32)], axis=1)
    st = jnp.concatenate([jnp.zeros((T, MLA_NOPE), F32), jnp.sin(ang) * sign,
                          jnp.zeros((T, HEAD_PAD - MLA_QK), F32)], axis=1)
    return ct, st


def _no_rope_tables(T):
    lane = np.arange(HEAD_PAD)
    ct = jnp.broadcast_to(jnp.asarray((lane < MLA_QK).astype(np.float32)), (T, HEAD_PAD))
    return ct, jnp.zeros((T, HEAD_PAD), F32)


def _tile(n, pref):
    t = min(n, pref)
    while n % t:
        t //= 2
    return t


def kernel(x, c, ctx, c_ctx, w_mod, b_mod, g_norm1, g_norm2, w_in, conv_w, conv_b, lru_w_a, lru_b_a,
           lru_w_x, lru_b_x, lru_lambda, g_q_lat, w_uq, g_kv_lat, w_ukv, g_qn, g_kn, w_pool, pool_scale,
           w_out, w_ff1, w_ff2):
    B, L, D = x.shape
    Lc = ctx.shape[1]
    depth = w_mod.shape[0]
    d_lru = conv_w.shape[2]

    rows = -(-(B + 1) // SUBLANES) * SUBLANES
    cvec = jnp.zeros((rows, D), F32).at[:B].set(c).at[B].set(c_ctx)
    mods = _mod_call(cvec, w_mod, b_mod)

    rope_l = _rope_tables(L)
    rope_c = _no_rope_tables(Lc)
    tm_l, tm_c = _tile(L, 512), _tile(Lc, 512)
    tp_l, tp_c = _tile(L, 512), _tile(Lc, 512)
    tq_l, tq_c = _tile(L, 2048), _tile(Lc, 2048)
    tc_l, tc_c = _tile(L, 256), _tile(Lc, 256)

    h = ctx
    zeros_state = jnp.zeros((B, 2, d_lru), F32)
    for l in range(depth):
        last = l == depth - 1
        lw = _layer_weights(l, w_in, conv_w, conv_b, lru_w_a, lru_b_a, lru_w_x, lru_b_x, lru_lambda,
                            g_q_lat, w_uq, g_kv_lat, w_ukv, g_qn, g_kn, w_pool, pool_scale, w_out,
                            w_ff1, w_ff2)
        ml = mods[l, :B].reshape(B, 1, 6, D)
        mc = jnp.broadcast_to(mods[l, B].reshape(1, 1, 6, D), (B, 1, 6, D))
        sh1, sc1, g1, sh2, sc2, g2 = (ml[:, :, i] for i in range(6))
        csh1, csc1, cg1, csh2, csc2, cg2 = (mc[:, :, i] for i in range(6))
        gn1 = g_norm1[l][None]
        gn2 = g_norm2[l][None]

        tab_c, tab_l = (
            (ct * lw["gq"], st * lw["gq_sw"], ct * lw["gk"], st * lw["gk_sw"]) for ct, st in (rope_c, rope_l))
        lx_c, lg_c, pu_c, q_c, k_c, v_c = _inproj_call(h, csh1, csc1, gn1, lw, tab_c, tm_c)
        a_c, p_c, hfin = _seq_call(lx_c, lg_c, pu_c, zeros_state, lw, tc_c)
        lx_l, lg_l, pu_l, q_l, k_l, v_l = _inproj_call(x, sh1, sc1, gn1, lw, tab_l, tm_l)
        a_l, p_l, _ = _seq_call(lx_l, lg_l, pu_l, hfin, lw, tc_l)
        att_l = _attn_call(q_l, [(k_c, v_c), (k_l, v_l)], tq_l)
        x = _post_call(x, a_l, att_l, p_l, g1, sh2, sc2, g2, gn2, lw, tp_l)
        if not last:
            att_c = _attn_call(q_c, [(k_c, v_c)], tq_c)
            h = _post_call(h, a_c, att_c, p_c, cg1, csh2, csc2, cg2, gn2, lw, tp_c)
    return x
```

```python
import functools
import math

import jax
import jax.numpy as jnp
import numpy as np
from jax import lax
from jax.experimental import pallas as pl
from jax.experimental.pallas import tpu as pltpu

F32 = jnp.float32
BF16 = jnp.bfloat16

EPS = 1e-6
GRID_W = 64
LRU_HEADS = 4
LRU_C = 8.0
CONV_W = 4
MLA_V = 64
MLA_NOPE = 64
MLA_ROPE = 32
MLA_QK = MLA_NOPE + MLA_ROPE
ROPE_FREQS = MLA_ROPE // 4
ROPE_BASE = 10000.0
POOL_WINDOWS = (2, 4, 8, 16)

LANES = 128
SUBLANES = 8
HEAD_PAD = LANES
BF16_ROWS = 16
V_ROWS = MLA_V + BF16_ROWS
HALO = 16
VMEM_LIMIT = 56 * 1024 * 1024

LOG2E = 1.4426950408889634


def _cparams(n_grid, flags=None):
    return pltpu.CompilerParams(dimension_semantics=("arbitrary",) * n_grid,
                                vmem_limit_bytes=VMEM_LIMIT, flags=flags)


def _const_spec(shape):
    zeros = (0,) * len(shape)
    return pl.BlockSpec(shape, lambda *_: zeros)


def _mod_kernel(c_ref, w_ref, b_ref, o_ref):
    c = c_ref[...]
    act = (c * jax.nn.sigmoid(c)).astype(BF16)
    o_ref[0] = jnp.dot(act, w_ref[0].astype(BF16), preferred_element_type=F32) + b_ref[0]


def _mod_call(cvec, w_mod, b_mod):
    depth, d, n = w_mod.shape
    rows = cvec.shape[0]
    tn = 1024
    return pl.pallas_call(
        _mod_kernel,
        grid=(depth, n // tn),
        in_specs=[pl.BlockSpec((rows, d), lambda l, j: (0, 0)),
                  pl.BlockSpec((1, d, tn), lambda l, j: (l, 0, j)),
                  pl.BlockSpec((1, 1, tn), lambda l, j: (l, 0, j))],
        out_specs=pl.BlockSpec((1, rows, tn), lambda l, j: (l, 0, j)),
        out_shape=jax.ShapeDtypeStruct((depth, rows, n), F32),
        compiler_params=_cparams(2),
        name="adaln_mod",
    )(cvec, w_mod, b_mod.reshape(depth, 1, n))


def _inproj_kernel(x_ref, sh_ref, sc_ref, g1_ref, win_ref, gql_ref, wuq_ref, gkvl_ref, wk_ref,
                   wv_ref, ones_ref, gains_ref, ct_ref, st_ref,
                   lx_ref, lg_ref, pu_ref, q_ref, k_ref, v_ref, *, d_lru, q_rank, kv_rank, heads):
    x = x_ref[0]
    xn = x * lax.rsqrt(jnp.mean(x * x, axis=-1, keepdims=True) + EPS)
    hx = (xn * g1_ref[...]) * (1.0 + sc_ref[0]) + sh_ref[0]
    z = jnp.dot(hx.astype(BF16), win_ref[...], preferred_element_type=F32)
    o = 0
    lx_ref[0] = z[:, o:o + d_lru]; o += d_lru
    lg_ref[0] = z[:, o:o + d_lru]; o += d_lru
    ql = z[:, o:o + q_rank]; o += q_rank
    kvl = z[:, o:o + kv_rank]; o += kv_rank
    krb = z[:, o:o + HEAD_PAD]; o += HEAD_PAD
    pu_ref[0] = z[:, o:]

    qln = ql * lax.rsqrt(jnp.mean(ql * ql, axis=-1, keepdims=True) + EPS) * gql_ref[...]
    qf = jnp.dot(qln.astype(BF16), wuq_ref[...], preferred_element_type=F32)
    kvn = kvl * lax.rsqrt(jnp.mean(kvl * kvl, axis=-1, keepdims=True) + EPS) * gkvl_ref[...]
    kvn = kvn.astype(BF16)
    kf = jnp.dot(kvn, wk_ref[...], preferred_element_type=F32)
    vf = jnp.dot(kvn, wv_ref[...], preferred_element_type=F32)

    lane = lax.broadcasted_iota(jnp.int32, (1, HEAD_PAD), 1)
    kr = jnp.where((lane >= MLA_NOPE) & (lane < MLA_QK), krb, 0.0)
    kr_sw = pltpu.roll(krb, HEAD_PAD - MLA_ROPE, axis=1)
    kr2 = jnp.concatenate([kr, kr], axis=1)
    ct, st = ct_ref[...], st_ref[...]
    gains = gains_ref[...]
    aq, bq, ak, bk = ct * gains[0:1], st * gains[1:2], ct * gains[2:3], st * gains[3:4]
    ones = ones_ref[...]
    inv_qk = 1.0 / MLA_QK
    n_main = heads * HEAD_PAD
    pair = 2 * HEAD_PAD
    for hp in range(heads // 2):
        qa = qf[:, hp * pair:(hp + 1) * pair]
        ss = jnp.dot((qa * qa).astype(BF16), ones, preferred_element_type=F32)
        rq = lax.rsqrt(ss * inv_qk + EPS)
        ka = kf[:, hp * pair:(hp + 1) * pair] + kr2
        ss = jnp.dot((ka * ka).astype(BF16), ones, preferred_element_type=F32)
        rk = lax.rsqrt(ss * inv_qk + EPS)
        for e in range(2):
            h = 2 * hp + e
            lo, hi = e * HEAD_PAD, (e + 1) * HEAD_PAD
            grp = qf[:, n_main + (h // 4) * HEAD_PAD:n_main + (h // 4 + 1) * HEAD_PAD]
            shift = (MLA_NOPE - (h % 4) * MLA_ROPE) % HEAD_PAD
            qb = pltpu.roll(grp, shift, axis=1) if shift else grp
            q_ref[0, h] = (rq[:, lo:hi] * (qa[:, lo:hi] * aq + qb * bq)).astype(BF16)
            k_ref[0, h] = (rk[:, lo:hi] * (ka[:, lo:hi] * ak + kr_sw * bk)).astype(BF16)
    tm = vf.shape[0]
    row = lax.broadcasted_iota(jnp.int32, (BF16_ROWS, tm), 0)
    extra = (row == 0).astype(F32)
    for p in range(heads // 2):
        vt = vf[:, p * LANES:(p + 1) * LANES].T
        v_ref[0, p] = jnp.concatenate([vt[:MLA_V], extra, vt[MLA_V:], extra], axis=0).astype(BF16)


def _inproj_call(x, sh, sc, g1, lw, tables, tm):
    B, T, D = x.shape
    pos = pl.BlockSpec((tm, HEAD_PAD), lambda b, i: (i, 0))
    heads = lw["heads"]
    d_lru, q_rank, kv_rank = lw["d_lru"], lw["q_rank"], lw["kv_rank"]
    d_pool = lw["w_in"].shape[1] - (2 * d_lru + q_rank + kv_rank + HEAD_PAD)
    tok = lambda w: pl.BlockSpec((1, tm, w), lambda b, i: (b, i, 0))
    vec = pl.BlockSpec((1, 1, D), lambda b, i: (b, 0, 0))
    hd = lambda n: pl.BlockSpec((1, n, tm, LANES), lambda b, i: (b, 0, i, 0))
    kern = functools.partial(_inproj_kernel, d_lru=d_lru, q_rank=q_rank, kv_rank=kv_rank, heads=heads)
    return pl.pallas_call(
        kern,
        grid=(B, T // tm),
        in_specs=[tok(D), vec, vec, _const_spec((1, D)), _const_spec(lw["w_in"].shape),
                  _const_spec((1, q_rank)), _const_spec(lw["w_uq"].shape),
                  _const_spec((1, kv_rank)), _const_spec(lw["w_k"].shape), _const_spec(lw["w_v"].shape),
                  _const_spec(lw["head_ones"].shape), _const_spec(lw["head_gains"].shape), pos, pos],
        out_specs=[tok(d_lru), tok(d_lru), tok(d_pool), hd(heads), hd(heads),
                   pl.BlockSpec((1, heads // 2, 2 * V_ROWS, tm), lambda b, i: (b, 0, 0, i))],
        out_shape=[jax.ShapeDtypeStruct((B, T, d_lru), F32),
                   jax.ShapeDtypeStruct((B, T, d_lru), F32),
                   jax.ShapeDtypeStruct((B, T, d_pool), F32),
                   jax.ShapeDtypeStruct((B, heads, T, LANES), BF16),
                   jax.ShapeDtypeStruct((B, heads, T, LANES), BF16),
                   jax.ShapeDtypeStruct((B, heads // 2, 2 * V_ROWS, T), BF16)],
        compiler_params=_cparams(2),
        name="in_proj",
    )(x, sh, sc, g1, lw["w_in"], lw["g_q_lat"], lw["w_uq"], lw["g_kv_lat"], lw["w_k"], lw["w_v"],
      lw["head_ones"], lw["head_gains"], *tables)


def _roll_rows(a, shift):
    return pltpu.roll(a, shift % a.shape[0], axis=0)


def _scan8(a, b, reverse):
    n = a.shape[0]
    a3 = a.reshape(n // SUBLANES, SUBLANES, a.shape[1])
    b3 = b.reshape(n // SUBLANES, SUBLANES, b.shape[1])
    rid = lax.broadcasted_iota(jnp.int32, a3.shape, 1)
    for s in (1, 2, 4):
        if reverse:
            m = rid < SUBLANES - s
            sh = SUBLANES - s
        else:
            m = rid >= s
            sh = s
        a_sh = pltpu.roll(a3, sh, axis=1)
        b_sh = pltpu.roll(b3, sh, axis=1)
        b3 = jnp.where(m, a3 * b_sh + b3, b3)
        a3 = jnp.where(m, a3 * a_sh, a3)
    return a3, b3


def _seq_kernel(lx_ref, lg_ref, pu_ref, h0_ref, cw_ref, cb_ref, wg_ref, bg_ref, lam_ref, wp_ref,
                ps_ref, a_out_ref, p_out_ref, hfin_ref, xpad_ref, upad_ref, hf_ref, xc_ref, *, tc):
    T = lx_ref.shape[1]
    C = lx_ref.shape[2]
    nc = T // tc
    zeros_halo = jnp.zeros((HALO, C), F32)
    xpad_ref[0:HALO, :] = zeros_halo
    xpad_ref[HALO + T:HALO + T + HALO, :] = zeros_halo
    upad_ref[0:HALO, :] = zeros_halo
    upad_ref[HALO + T:HALO + T + HALO, :] = zeros_halo

    def fill(c, carry):
        r0 = pl.multiple_of(c * tc, tc)
        xpad_ref[pl.ds(HALO + r0, tc), :] = lx_ref[0, pl.ds(r0, tc), :]
        upad_ref[pl.ds(HALO + r0, tc), :] = pu_ref[0, pl.ds(r0, tc), :]
        return carry

    lax.fori_loop(0, nc, fill, 0)

    lam = lam_ref[...]
    coef = -LRU_C * (jnp.maximum(-lam, 0.0) + jnp.log(1.0 + jnp.exp(-jnp.abs(lam))))
    cw = cw_ref[...]
    cb = cb_ref[...]
    bg = bg_ref[...]

    def conv_chunk(r0):
        w = xpad_ref[pl.ds(r0 + HALO - SUBLANES, tc + 2 * SUBLANES), :]
        y = (cw[0:1] * _roll_rows(w, 1) + cw[1:2] * w + cw[2:3] * _roll_rows(w, -1)
             + cw[3:4] * _roll_rows(w, -2))
        return y[SUBLANES:SUBLANES + tc] + cb

    def coeffs(xc, d):
        g = jnp.dot(xc.astype(BF16), wg_ref[:, 2 * C * d:2 * C * (d + 1)],
                    preferred_element_type=F32) + bg[:, 2 * C * d:2 * C * (d + 1)]
        r = jax.nn.sigmoid(g[:, :C])
        i = jax.nn.sigmoid(g[:, C:])
        log_a = coef[d:d + 1] * r
        a = jnp.exp(log_a)
        b = jnp.exp2(0.5 * jnp.log2(1.0 - a * a)) * (i * xc)
        return a, b

    lane = lax.broadcasted_iota(jnp.int32, (1, C), 1)
    gdim = C // len(POOL_WINDOWS)

    inv_win = None
    for g, win in enumerate(POOL_WINDOWS):
        inv_g = jnp.full((1, C), 1.0 / win, F32)
        inv_win = inv_g if inv_win is None else jnp.where(lane >= g * gdim, inv_g, inv_win)

    def pool_rows(r0, n, edge):
        w = upad_ref[pl.ds(r0, n + 2 * HALO), :]
        s2 = w + _roll_rows(w, 1)
        s4 = _roll_rows(s2, 1) + _roll_rows(s2, -1)
        s8 = _roll_rows(s4, 2) + _roll_rows(s4, -2)
        s16 = _roll_rows(s8, 4) + _roll_rows(s8, -4)
        sel = s2[HALO:HALO + n]
        for g, s in enumerate((s4, s8, s16), start=1):
            sel = jnp.where(lane >= g * gdim, s[HALO:HALO + n], sel)
        if edge:
            t = r0 + lax.broadcasted_iota(jnp.int32, (n, 1), 0)
            cnt = None
            for g, win in enumerate(POOL_WINDOWS):
                cg = (jnp.minimum(t - win // 2 + win, T) - jnp.maximum(t - win // 2, 0)).astype(F32)
                cg = jnp.broadcast_to(cg, (n, C))
                cnt = cg if cnt is None else jnp.where(lane >= g * gdim, cg, cnt)
            mean = sel / cnt
        else:
            mean = sel * inv_win
        mixed = mean - w[HALO:HALO + n]
        y = jnp.dot(mixed.astype(BF16), wp_ref[...], preferred_element_type=F32) * ps_ref[...]
        p_out_ref[0, pl.ds(r0, n), :] = y.astype(p_out_ref.dtype)

    nv = tc // SUBLANES

    def fwd(c, h):
        r0 = pl.multiple_of(c * tc, tc)
        xc = conv_chunk(r0)
        xc_ref[pl.ds(r0, tc), :] = xc
        a, b = coeffs(xc, 0)
        a3, b3 = _scan8(a, b, reverse=False)
        for j in range(nv):
            hj = b3[j] + a3[j] * h
            hf_ref[pl.ds(r0 + j * SUBLANES, SUBLANES), :] = hj
            h = jnp.broadcast_to(hj[SUBLANES - 1:SUBLANES], (SUBLANES, C))
        return h

    unroll = 2 if nc % 2 == 0 else 1
    h = lax.fori_loop(0, nc, fwd, jnp.broadcast_to(h0_ref[0, 0:1, :], (SUBLANES, C)), unroll=unroll)
    hfin_ref[0, 0:1, :] = h[0:1]

    def bwd(cc, h):
        c = nc - 1 - cc
        r0 = pl.multiple_of(c * tc, tc)
        pool_rows(r0, tc, False)
        a, b = coeffs(xc_ref[pl.ds(r0, tc), :], 1)
        a3, b3 = _scan8(a, b, reverse=True)
        for j in range(nv - 1, -1, -1):
            hj = b3[j] + a3[j] * h
            rows = pl.ds(r0 + j * SUBLANES, SUBLANES)
            y = hj + hf_ref[rows, :]
            a_out_ref[0, rows, :] = (y * jax.nn.gelu(lg_ref[0, rows, :])).astype(a_out_ref.dtype)
            h = jnp.broadcast_to(hj[0:1], (SUBLANES, C))
        return h

    h = lax.fori_loop(0, nc, bwd, jnp.broadcast_to(h0_ref[0, 1:2, :], (SUBLANES, C)), unroll=unroll)
    hfin_ref[0, 1:2, :] = h[0:1]

    pool_rows(0, BF16_ROWS, True)
    pool_rows(T - BF16_ROWS, BF16_ROWS, True)


def _seq_call(lx, lg, pu, h0, lw, tc):
    B, T, C = lx.shape
    seq = pl.BlockSpec((1, T, C), lambda b: (b, 0, 0))
    st = pl.BlockSpec((1, 2, C), lambda b: (b, 0, 0))
    return pl.pallas_call(
        functools.partial(_seq_kernel, tc=tc),
        grid=(B,),
        in_specs=[seq, seq, seq, st, _const_spec((CONV_W, C)), _const_spec((1, C)),
                  _const_spec((C, 4 * C)), _const_spec((1, 4 * C)), _const_spec((2, C)),
                  _const_spec((C, C)), _const_spec((1, C))],
        out_specs=[seq, seq, st],
        out_shape=[jax.ShapeDtypeStruct((B, T, C), BF16), jax.ShapeDtypeStruct((B, T, C), BF16),
                   jax.ShapeDtypeStruct((B, 2, C), F32)],
        scratch_shapes=[pltpu.VMEM((T + 2 * HALO, C), F32), pltpu.VMEM((T + 2 * HALO, C), F32),
                        pltpu.VMEM((T, C), F32), pltpu.VMEM((T, C), F32)],
        compiler_params=_cparams(1),
        name="seq_mix",
    )(lx, lg, pu, h0, lw["conv_w"], lw["conv_b"], lw["w_gate"], lw["b_gate"], lw["lam"],
      lw["w_pool"], lw["pool_scale"])


def _attn_kernel(*refs, n_seg, tq_sub, kc):
    q_ref = refs[0]
    kv_refs = refs[1:1 + 2 * n_seg]
    o_ref = refs[1 + 2 * n_seg]
    s_scr = refs[2 + 2 * n_seg]
    n_sub = q_ref.shape[2] // tq_sub
    units = [(hh, j) for j in range(n_sub) for hh in range(2)]
    chunks = []
    off = 0
    for seg in range(n_seg):
        lk = kv_refs[2 * seg].shape[2]
        for st in range(0, lk, kc):
            sz = min(kc, lk - st)
            chunks.append((seg, st, sz, off))
            off += sz
    nch = len(chunks)
    mvec = {}
    col_max = {}
    acc = {}
    results = {}
    pv_hist = []
    depth = 5

    def score_chunk(u, c):
        hh, j = units[u]
        seg, st, sz, off = chunks[c]
        q = q_ref[0, hh, j * tq_sub:(j + 1) * tq_sub, :]
        k = kv_refs[2 * seg][0, hh, st:st + sz, :]
        s = lax.dot_general(k, q, (((1,), (1,)), ((), ())), preferred_element_type=F32)
        s_scr[u % 2, off:off + sz, :] = s
        cm = jnp.max(s.reshape(sz // SUBLANES, SUBLANES, tq_sub), axis=0)
        mvec[u] = cm if c == 0 else jnp.maximum(mvec[u], cm)
        if c == nch - 1:
            col_max[u] = jnp.max(mvec[u], axis=0, keepdims=True)

    def weight_chunk(u, c):
        hh, j = units[u]
        seg, st, sz, off = chunks[c]
        m = col_max[u]
        if len(pv_hist) >= depth:
            bits = pltpu.bitcast(pv_hist[-depth][MLA_V:MLA_V + 1], jnp.uint32)
            m = m + ((bits >> 16) >> 16).astype(F32)
        p = jnp.exp2((s_scr[u % 2, off:off + sz, :] - m).astype(BF16))
        vt = kv_refs[2 * seg + 1][0, 0, hh * V_ROWS:(hh + 1) * V_ROWS, st:st + sz]
        pv = jnp.dot(vt, p, preferred_element_type=F32)
        pv_hist.append(pv)
        acc[u] = pv if c == 0 else acc[u] + pv
        if c == nch - 1:
            results[units[u]] = acc[u][:MLA_V] / acc[u][MLA_V:MLA_V + 1]

    lead = nch + min(4, nch - 1)
    events = []
    for u in range(len(units)):
        for c in range(nch):
            events.append((u * nch + c - lead, 0, u, c))
            events.append((u * nch + c, 1, u, c))
    for _, kind, u, c in sorted(events):
        (weight_chunk if kind else score_chunk)(u, c)

    for j in range(n_sub):
        o_t = jnp.concatenate([results[(0, j)], results[(1, j)]], axis=0)
        o_ref[0, j * tq_sub:(j + 1) * tq_sub, :] = o_t.T.astype(o_ref.dtype)


def _attn_call(q, segs, tq):
    B, H, T, _ = q.shape
    tq_sub = min(tq, 256)
    in_specs = [pl.BlockSpec((1, 2, tq, LANES), lambda b, p, i: (b, p, i, 0))]
    args = [q]
    lk_total = 0
    for k, v in segs:
        lk = k.shape[2]
        lk_total += lk
        in_specs.append(pl.BlockSpec((1, 2, lk, LANES), lambda b, p, i: (b, p, 0, 0)))
        in_specs.append(pl.BlockSpec((1, 1, 2 * V_ROWS, lk), lambda b, p, i: (b, p, 0, 0)))
        args += [k, v]
    return pl.pallas_call(
        functools.partial(_attn_kernel, n_seg=len(segs), tq_sub=tq_sub, kc=256),
        grid=(B, H // 2, T // tq),
        in_specs=in_specs,
        out_specs=pl.BlockSpec((1, tq, LANES), lambda b, p, i: (b, i, p)),
        out_shape=jax.ShapeDtypeStruct((B, T, H * MLA_V), BF16),
        scratch_shapes=[pltpu.VMEM((2, lk_total, tq_sub), F32)],
        compiler_params=_cparams(3),
        name="attention",
    )(*args)


def _post_kernel(x_ref, a_ref, att_ref, p_ref, g1_ref, sh_ref, sc_ref, g2_ref, gn_ref, wo_ref,
                 w1_ref, w2_ref, o_ref, *, ff_chunk):
    d_lru = a_ref.shape[2]
    d_att = att_ref.shape[2]
    y = jnp.dot(a_ref[0], wo_ref[0:d_lru, :], preferred_element_type=F32)
    y += jnp.dot(att_ref[0], wo_ref[d_lru:d_lru + d_att, :], preferred_element_type=F32)
    y += jnp.dot(p_ref[0], wo_ref[d_lru + d_att:, :], preferred_element_type=F32)
    x1 = x_ref[0] + g1_ref[0] * y
    xn = x1 * lax.rsqrt(jnp.mean(x1 * x1, axis=-1, keepdims=True) + EPS)
    h = ((xn * gn_ref[...]) * (1.0 + sc_ref[0]) + sh_ref[0]).astype(BF16)
    d_ff = w1_ref.shape[1]
    acc = None
    for c in range(d_ff // ff_chunk):
        u = jnp.dot(h, w1_ref[:, c * ff_chunk:(c + 1) * ff_chunk], preferred_element_type=F32)
        u = jnp.square(jnp.maximum(u, 0.0)).astype(BF16)
        part = jnp.dot(u, w2_ref[c * ff_chunk:(c + 1) * ff_chunk, :], preferred_element_type=F32)
        acc = part if acc is None else acc + part
    o_ref[0] = x1 + g2_ref[0] * acc


def _post_call(x, a, att, p, g1, sh, sc, g2, gn, lw, tm):
    B, T, D = x.shape
    tok = lambda w: pl.BlockSpec((1, tm, w), lambda b, i: (b, i, 0))
    vec = pl.BlockSpec((1, 1, D), lambda b, i: (b, 0, 0))
    single = lambda shape: pl.BlockSpec(shape, lambda b, i: (0,) * len(shape),
                                        pipeline_mode=pl.Buffered(1))
    return pl.pallas_call(
        functools.partial(_post_kernel, ff_chunk=1024),
        grid=(B, T // tm),
        in_specs=[tok(D), tok(a.shape[2]), tok(att.shape[2]), tok(p.shape[2]), vec, vec, vec, vec,
                  _const_spec((1, D)), single(lw["w_out"].shape), single(lw["w_ff1"].shape),
                  single(lw["w_ff2"].shape)],
        out_specs=tok(D),
        out_shape=jax.ShapeDtypeStruct((B, T, D), F32),
        compiler_params=_cparams(2),
        name="out_proj_mlp",
    )(x, a, att, p, g1, sh, sc, g2, gn, lw["w_out"], lw["w_ff1"], lw["w_ff2"])


def _rope_partner():
    j = np.arange(MLA_ROPE)
    half = (j % (2 * ROPE_FREQS)) // ROPE_FREQS
    return np.where(half == 0, j + ROPE_FREQS, j - ROPE_FREQS)


def _block_diag(w):
    n, a, b = w.shape
    eye = jnp.asarray(np.eye(n, dtype=np.float32), w.dtype)
    return (w[:, :, None, :] * eye[:, None, :, None]).reshape(n * a, n * b)


def _layer_weights(l, w_in, conv_w, conv_b, lru_w_a, lru_b_a, lru_w_x, lru_b_x, lru_lambda, g_q_lat,
                   w_uq, g_kv_lat, w_ukv, g_qn, g_kn, w_pool, pool_scale, w_out, w_ff1, w_ff2):
    d = w_in.shape[1]
    d_lru = conv_w.shape[2]
    q_rank = w_uq.shape[1]
    kv_rank = w_ukv.shape[1]
    heads = w_uq.shape[2] // MLA_QK
    partner = _rope_partner()
    o_kr = 2 * d_lru + q_rank + kv_rank
    wi = w_in[l]
    kr = wi[:, o_kr:o_kr + MLA_ROPE]
    krb = jnp.concatenate([jnp.zeros((d, MLA_NOPE), F32), kr, kr[:, partner]], axis=1)
    w_in_p = jnp.concatenate([wi[:, :o_kr], krb, wi[:, o_kr + MLA_ROPE:]], axis=1).astype(BF16)
    wq = w_uq[l].reshape(q_rank, heads, MLA_QK)
    pad = jnp.zeros((q_rank, heads, HEAD_PAD - MLA_QK), F32)
    wq_main = jnp.concatenate([wq, pad], axis=2)
    wq_partner = wq[:, :, MLA_NOPE + partner]
    wq = jnp.concatenate([wq_main.reshape(q_rank, heads * HEAD_PAD),
                          wq_partner.reshape(q_rank, heads * MLA_ROPE)], axis=1)
    wkv = w_ukv[l].reshape(kv_rank, heads, MLA_NOPE + MLA_V)
    wk = jnp.concatenate([wkv[:, :, :MLA_NOPE], jnp.zeros((kv_rank, heads, HEAD_PAD - MLA_NOPE), F32)],
                         axis=2).reshape(kv_rank, heads * HEAD_PAD)
    wv = wkv[:, :, MLA_NOPE:].reshape(kv_rank, heads * MLA_V)
    zpad = jnp.zeros((HEAD_PAD - MLA_QK,), F32)
    znope = jnp.zeros((MLA_NOPE,), F32)
    q_scale = LOG2E / math.sqrt(MLA_QK)
    gq = jnp.concatenate([g_qn[l], zpad]) * q_scale
    gq_sw = jnp.concatenate([znope, g_qn[l][MLA_NOPE + partner], zpad]) * q_scale
    gk = jnp.concatenate([g_kn[l], zpad])
    gk_sw = jnp.concatenate([znope, g_kn[l][MLA_NOPE + partner], zpad])
    head_ones = jnp.asarray(np.kron(np.eye(2), np.ones((HEAD_PAD, HEAD_PAD))), BF16)
    w_gate = jnp.concatenate([_block_diag(lru_w_a[l, 0]), _block_diag(lru_w_x[l, 0]),
                              _block_diag(lru_w_a[l, 1]), _block_diag(lru_w_x[l, 1])], axis=1)
    b_gate = jnp.concatenate([lru_b_a[l, 0], lru_b_x[l, 0], lru_b_a[l, 1], lru_b_x[l, 1]])
    return dict(
        heads=heads, d_lru=d_lru, q_rank=q_rank, kv_rank=kv_rank,
        w_in=w_in_p, g_q_lat=g_q_lat[l][None], w_uq=wq.astype(BF16), g_kv_lat=g_kv_lat[l][None],
        w_k=wk.astype(BF16), w_v=wv.astype(BF16), head_gains=jnp.stack([gq, gq_sw, gk, gk_sw]),
        head_ones=head_ones,
        conv_w=conv_w[l], conv_b=conv_b[l][None], w_gate=w_gate.astype(BF16), b_gate=b_gate[None],
        lam=lru_lambda[l], w_pool=_block_diag(w_pool[l]).astype(BF16), pool_scale=pool_scale[l][None],
        w_out=w_out[l].astype(BF16), w_ff1=w_ff1[l].astype(BF16), w_ff2=w_ff2[l].astype(BF16))


def _rope_tables(T):
    f32 = np.float32
    j = np.arange(MLA_ROPE)
    axis = j // (2 * ROPE_FREQS)
    half = (j % (2 * ROPE_FREQS)) // ROPE_FREQS
    f = j % ROPE_FREQS
    t = np.arange(T)
    pos = np.stack([(t // GRID_W).astype(f32), (t % GRID_W).astype(f32)], axis=-1)
    freqs = np.power(f32(ROPE_BASE), -np.arange(ROPE_FREQS, dtype=f32) / f32(ROPE_FREQS)).astype(f32)
    ang = (pos[:, axis] * freqs[f]).astype(f32)
    sign = np.where(half == 0, -1.0, 1.0).astype(f32)
    ct = np.concatenate([np.ones((T, MLA_NOPE), f32), np.cos(ang).astype(f32),
                         np.zeros((T, HEAD_PAD - MLA_QK), f32)], axis=1)
    st = np.concatenate([np.zeros((T, MLA_NOPE), f32), (np.sin(ang) * sign).astype(f32),
                         np.zeros((T, HEAD_PAD - MLA_QK), f32)], axis=1)
    return jnp.asarray(ct), jnp.asarray(st)


def _no_rope_tables(T):
    lane = np.arange(HEAD_PAD)
    ct = np.broadcast_to((lane < MLA_QK).astype(np.float32), (T, HEAD_PAD))
    return jnp.asarray(ct), jnp.zeros((T, HEAD_PAD), F32)


def _tile(n, pref):
    t = min(n, pref)
    while n % t:
        t //= 2
    return t


def kernel(x, c, ctx, c_ctx, w_mod, b_mod, g_norm1, g_norm2, w_in, conv_w, conv_b, lru_w_a, lru_b_a,
           lru_w_x, lru_b_x, lru_lambda, g_q_lat, w_uq, g_kv_lat, w_ukv, g_qn, g_kn, w_pool, pool_scale,
           w_out, w_ff1, w_ff2):
    B, L, D = x.shape
    Lc = ctx.shape[1]
    depth = w_mod.shape[0]
    d_lru = conv_w.shape[2]

    rows = -(-(B + 1) // SUBLANES) * SUBLANES
    cvec = jnp.concatenate([c, c_ctx[None], jnp.zeros((rows - B - 1, D), F32)], axis=0)
    mods = _mod_call(cvec, w_mod, b_mod)

    rope_l = _rope_tables(L)
    rope_c = _no_rope_tables(Lc)
    tm_l, tm_c = _tile(L, 512), _tile(Lc, 512)
    tp_l, tp_c = _tile(L, 1024), _tile(Lc, 1024)
    tq_l, tq_c = _tile(L, 2048), _tile(Lc, 2048)
    tc_l, tc_c = _tile(L, 256), _tile(Lc, 256)

    h = ctx
    zeros_state = jnp.zeros((B, 2, d_lru), F32)
    for l in range(depth):
        last = l == depth - 1
        lw = _layer_weights(l, w_in, conv_w, conv_b, lru_w_a, lru_b_a, lru_w_x, lru_b_x, lru_lambda,
                            g_q_lat, w_uq, g_kv_lat, w_ukv, g_qn, g_kn, w_pool, pool_scale, w_out,
                            w_ff1, w_ff2)
        ml = mods[l, :B].reshape(B, 1, 6, D)
        mc = jnp.broadcast_to(mods[l, B].reshape(1, 1, 6, D), (B, 1, 6, D))
        sh1, sc1, g1, sh2, sc2, g2 = (ml[:, :, i] for i in range(6))
        csh1, csc1, cg1, csh2, csc2, cg2 = (mc[:, :, i] for i in range(6))
        gn1 = g_norm1[l][None]
        gn2 = g_norm2[l][None]

        lx_c, lg_c, pu_c, q_c, k_c, v_c = _inproj_call(h, csh1, csc1, gn1, lw, rope_c, tm_c)
        a_c, p_c, hfin = _seq_call(lx_c, lg_c, pu_c, zeros_state, lw, tc_c)
        lx_l, lg_l, pu_l, q_l, k_l, v_l = _inproj_call(x, sh1, sc1, gn1, lw, rope_l, tm_l)
        a_l, p_l, _ = _seq_call(lx_l, lg_l, pu_l, hfin, lw, tc_l)
        att_l = _attn_call(q_l, [(k_c, v_c), (k_l, v_l)], tq_l)
        x = _post_call(x, a_l, att_l, p_l, g1, sh2, sc2, g2, gn2, lw, tp_l)
        if not last:
            att_c = _attn_call(q_c, [(k_c, v_c)], tq_c)
            h = _post_call(h, a_c, att_c, p_c, cg1, csh2, csc2, cg2, gn2, lw, tp_c)
    return x
```

```python
import functools
import math

import jax
import jax.numpy as jnp
import numpy as np
from jax import lax
from jax.experimental import pallas as pl
from jax.experimental.pallas import tpu as pltpu

F32 = jnp.float32
BF16 = jnp.bfloat16

EPS = 1e-6
GRID_W = 64
LRU_HEADS = 4
LRU_C = 8.0
CONV_W = 4
MLA_V = 64
MLA_NOPE = 64
MLA_ROPE = 32
MLA_QK = MLA_NOPE + MLA_ROPE
ROPE_FREQS = MLA_ROPE // 4
ROPE_BASE = 10000.0
POOL_WINDOWS = (2, 4, 8, 16)

LANES = 128
SUBLANES = 8
HEAD_PAD = LANES
BF16_ROWS = 16
V_ROWS = MLA_V + BF16_ROWS
HALO = 16
VMEM_LIMIT = 56 * 1024 * 1024

LOG2E = 1.4426950408889634


def _cparams(n_grid, flags=None):
    return pltpu.CompilerParams(dimension_semantics=("arbitrary",) * n_grid,
                                vmem_limit_bytes=VMEM_LIMIT, flags=flags)


def _const_spec(shape):
    zeros = (0,) * len(shape)
    return pl.BlockSpec(shape, lambda *_: zeros)


def _mod_kernel(c_ref, w_ref, b_ref, o_ref):
    c = c_ref[...]
    act = (c * jax.nn.sigmoid(c)).astype(BF16)
    o_ref[0] = jnp.dot(act, w_ref[0].astype(BF16), preferred_element_type=F32) + b_ref[0]


def _mod_call(cvec, w_mod, b_mod):
    depth, d, n = w_mod.shape
    rows = cvec.shape[0]
    tn = 1024
    return pl.pallas_call(
        _mod_kernel,
        grid=(depth, n // tn),
        in_specs=[pl.BlockSpec((rows, d), lambda l, j: (0, 0)),
                  pl.BlockSpec((1, d, tn), lambda l, j: (l, 0, j)),
                  pl.BlockSpec((1, 1, tn), lambda l, j: (l, 0, j))],
        out_specs=pl.BlockSpec((1, rows, tn), lambda l, j: (l, 0, j)),
        out_shape=jax.ShapeDtypeStruct((depth, rows, n), F32),
        compiler_params=_cparams(2),
        name="adaln_mod",
    )(cvec, w_mod, b_mod.reshape(depth, 1, n))


def _inproj_kernel(x_ref, sh_ref, sc_ref, g1_ref, win_ref, gql_ref, wuq_ref, gkvl_ref, wk_ref,
                   wv_ref, ones_ref, gains_ref, ct_ref, st_ref,
                   lx_ref, lg_ref, pu_ref, q_ref, k_ref, v_ref, *, d_lru, q_rank, kv_rank, heads, rsub):
    lane = lax.broadcasted_iota(jnp.int32, (1, HEAD_PAD), 1)
    rot = (lane >= MLA_NOPE) & (lane < MLA_QK)
    gains = gains_ref[...]
    ones = ones_ref[...]
    inv_qk = 1.0 / MLA_QK
    n_main = heads * HEAD_PAD
    pair = 2 * HEAD_PAD
    extra = (lax.broadcasted_iota(jnp.int32, (BF16_ROWS, rsub), 0) == 0).astype(F32)

    for r0 in range(0, x_ref.shape[1], rsub):
        rows = slice(r0, r0 + rsub)
        x = x_ref[0, rows, :]
        xn = x * lax.rsqrt(jnp.mean(x * x, axis=-1, keepdims=True) + EPS)
        hx = (xn * g1_ref[...]) * (1.0 + sc_ref[0]) + sh_ref[0]
        z = jnp.dot(hx.astype(BF16), win_ref[...], preferred_element_type=F32)
        o = 0
        lx_ref[0, rows, :] = z[:, o:o + d_lru]; o += d_lru
        lg_ref[0, rows, :] = z[:, o:o + d_lru]; o += d_lru
        ql = z[:, o:o + q_rank]; o += q_rank
        kvl = z[:, o:o + kv_rank]; o += kv_rank
        krb = z[:, o:o + HEAD_PAD]; o += HEAD_PAD
        pu_ref[0, rows, :] = z[:, o:]

        qln = ql * lax.rsqrt(jnp.mean(ql * ql, axis=-1, keepdims=True) + EPS) * gql_ref[...]
        qf = jnp.dot(qln.astype(BF16), wuq_ref[...], preferred_element_type=F32)
        kvn = kvl * lax.rsqrt(jnp.mean(kvl * kvl, axis=-1, keepdims=True) + EPS) * gkvl_ref[...]
        kvn = kvn.astype(BF16)
        kf = jnp.dot(kvn, wk_ref[...], preferred_element_type=F32)
        vf = jnp.dot(kvn, wv_ref[...], preferred_element_type=F32)

        kr = jnp.where(rot, krb, 0.0)
        kr_sw = pltpu.roll(krb, HEAD_PAD - MLA_ROPE, axis=1)
        kr2 = jnp.concatenate([kr, kr], axis=1)
        ct, st = ct_ref[rows, :], st_ref[rows, :]
        aq, bq, ak, bk = ct * gains[0:1], st * gains[1:2], ct * gains[2:3], st * gains[3:4]
        for hp in range(heads // 2):
            qa = qf[:, hp * pair:(hp + 1) * pair]
            ss = jnp.dot((qa * qa).astype(BF16), ones, preferred_element_type=F32)
            rq = lax.rsqrt(ss * inv_qk + EPS)
            ka = kf[:, hp * pair:(hp + 1) * pair] + kr2
            ss = jnp.dot((ka * ka).astype(BF16), ones, preferred_element_type=F32)
            rk = lax.rsqrt(ss * inv_qk + EPS)
            for e in range(2):
                h = 2 * hp + e
                lo, hi = e * HEAD_PAD, (e + 1) * HEAD_PAD
                grp = qf[:, n_main + (h // 4) * HEAD_PAD:n_main + (h // 4 + 1) * HEAD_PAD]
                shift = (MLA_NOPE - (h % 4) * MLA_ROPE) % HEAD_PAD
                qb = pltpu.roll(grp, shift, axis=1) if shift else grp
                q_ref[0, h, rows, :] = (rq[:, lo:hi] * (qa[:, lo:hi] * aq + qb * bq)).astype(BF16)
                k_ref[0, h, rows, :] = (rk[:, lo:hi] * (ka[:, lo:hi] * ak + kr_sw * bk)).astype(BF16)
        for p in range(heads // 2):
            vt = vf[:, p * LANES:(p + 1) * LANES].T
            v_ref[0, p, :, rows] = jnp.concatenate(
                [vt[:MLA_V], extra, vt[MLA_V:], extra], axis=0).astype(BF16)


def _inproj_call(x, sh, sc, g1, lw, tables, tm):
    B, T, D = x.shape
    pos = pl.BlockSpec((tm, HEAD_PAD), lambda b, i: (i, 0))
    heads = lw["heads"]
    d_lru, q_rank, kv_rank = lw["d_lru"], lw["q_rank"], lw["kv_rank"]
    d_pool = lw["w_in"].shape[1] - (2 * d_lru + q_rank + kv_rank + HEAD_PAD)
    tok = lambda w: pl.BlockSpec((1, tm, w), lambda b, i: (b, i, 0))
    vec = pl.BlockSpec((1, 1, D), lambda b, i: (b, 0, 0))
    hd = lambda n: pl.BlockSpec((1, n, tm, LANES), lambda b, i: (b, 0, i, 0))
    kern = functools.partial(_inproj_kernel, d_lru=d_lru, q_rank=q_rank, kv_rank=kv_rank, heads=heads,
                             rsub=min(tm, 256))
    return pl.pallas_call(
        kern,
        grid=(B, T // tm),
        in_specs=[tok(D), vec, vec, _const_spec((1, D)), _const_spec(lw["w_in"].shape),
                  _const_spec((1, q_rank)), _const_spec(lw["w_uq"].shape),
                  _const_spec((1, kv_rank)), _const_spec(lw["w_k"].shape), _const_spec(lw["w_v"].shape),
                  _const_spec(lw["head_ones"].shape), _const_spec(lw["head_gains"].shape), pos, pos],
        out_specs=[tok(d_lru), tok(d_lru), tok(d_pool), hd(heads), hd(heads),
                   pl.BlockSpec((1, heads // 2, 2 * V_ROWS, tm), lambda b, i: (b, 0, 0, i))],
        out_shape=[jax.ShapeDtypeStruct((B, T, d_lru), F32),
                   jax.ShapeDtypeStruct((B, T, d_lru), F32),
                   jax.ShapeDtypeStruct((B, T, d_pool), F32),
                   jax.ShapeDtypeStruct((B, heads, T, LANES), BF16),
                   jax.ShapeDtypeStruct((B, heads, T, LANES), BF16),
                   jax.ShapeDtypeStruct((B, heads // 2, 2 * V_ROWS, T), BF16)],
        compiler_params=_cparams(2),
        name="in_proj",
    )(x, sh, sc, g1, lw["w_in"], lw["g_q_lat"], lw["w_uq"], lw["g_kv_lat"], lw["w_k"], lw["w_v"],
      lw["head_ones"], lw["head_gains"], *tables)


def _roll_rows(a, shift):
    return pltpu.roll(a, shift % a.shape[0], axis=0)


def _scan8(a, b, reverse):
    n = a.shape[0]
    a3 = a.reshape(n // SUBLANES, SUBLANES, a.shape[1])
    b3 = b.reshape(n // SUBLANES, SUBLANES, b.shape[1])
    rid = lax.broadcasted_iota(jnp.int32, a3.shape, 1)
    for s in (1, 2, 4):
        if reverse:
            m = rid < SUBLANES - s
            sh = SUBLANES - s
        else:
            m = rid >= s
            sh = s
        a_sh = pltpu.roll(a3, sh, axis=1)
        b_sh = pltpu.roll(b3, sh, axis=1)
        b3 = jnp.where(m, a3 * b_sh + b3, b3)
        a3 = jnp.where(m, a3 * a_sh, a3)
    return a3, b3


def _seq_kernel(lx_ref, lg_ref, pu_ref, h0_ref, cw_ref, cb_ref, wg_ref, bg_ref, lam_ref, wp_ref,
                ps_ref, a_out_ref, p_out_ref, hfin_ref, xpad_ref, upad_ref, hf_ref, xc_ref, *, tc):
    T = lx_ref.shape[1]
    C = lx_ref.shape[2]
    nc = T // tc
    zeros_halo = jnp.zeros((HALO, C), F32)
    xpad_ref[0:HALO, :] = zeros_halo
    xpad_ref[HALO + T:HALO + T + HALO, :] = zeros_halo
    upad_ref[0:HALO, :] = zeros_halo
    upad_ref[HALO + T:HALO + T + HALO, :] = zeros_halo

    def fill(c, carry):
        r0 = pl.multiple_of(c * tc, tc)
        xpad_ref[pl.ds(HALO + r0, tc), :] = lx_ref[0, pl.ds(r0, tc), :]
        upad_ref[pl.ds(HALO + r0, tc), :] = pu_ref[0, pl.ds(r0, tc), :]
        return carry

    lax.fori_loop(0, nc, fill, 0)

    lam = lam_ref[...]
    coef = -LRU_C * (jnp.maximum(-lam, 0.0) + jnp.log(1.0 + jnp.exp(-jnp.abs(lam))))
    cw = cw_ref[...]
    cb = cb_ref[...]
    bg = bg_ref[...]

    def conv_chunk(r0):
        w = xpad_ref[pl.ds(r0 + HALO - SUBLANES, tc + 2 * SUBLANES), :]
        y = (cw[0:1] * _roll_rows(w, 1) + cw[1:2] * w + cw[2:3] * _roll_rows(w, -1)
             + cw[3:4] * _roll_rows(w, -2))
        return y[SUBLANES:SUBLANES + tc] + cb

    def coeffs(xc, d):
        g = jnp.dot(xc.astype(BF16), wg_ref[:, 2 * C * d:2 * C * (d + 1)],
                    preferred_element_type=F32) + bg[:, 2 * C * d:2 * C * (d + 1)]
        r = jax.nn.sigmoid(g[:, :C])
        i = jax.nn.sigmoid(g[:, C:])
        log_a = coef[d:d + 1] * r
        a = jnp.exp(log_a)
        b = jnp.exp2(0.5 * jnp.log2(1.0 - a * a)) * (i * xc)
        return a, b

    lane = lax.broadcasted_iota(jnp.int32, (1, C), 1)
    gdim = C // len(POOL_WINDOWS)

    inv_win = None
    for g, win in enumerate(POOL_WINDOWS):
        inv_g = jnp.full((1, C), 1.0 / win, F32)
        inv_win = inv_g if inv_win is None else jnp.where(lane >= g * gdim, inv_g, inv_win)

    def pool_rows(r0, n, edge):
        w = upad_ref[pl.ds(r0, n + 2 * HALO), :]
        s2 = w + _roll_rows(w, 1)
        s4 = _roll_rows(s2, 1) + _roll_rows(s2, -1)
        s8 = _roll_rows(s4, 2) + _roll_rows(s4, -2)
        s16 = _roll_rows(s8, 4) + _roll_rows(s8, -4)
        sel = s2[HALO:HALO + n]
        for g, s in enumerate((s4, s8, s16), start=1):
            sel = jnp.where(lane >= g * gdim, s[HALO:HALO + n], sel)
        if edge:
            t = r0 + lax.broadcasted_iota(jnp.int32, (n, 1), 0)
            cnt = None
            for g, win in enumerate(POOL_WINDOWS):
                cg = (jnp.minimum(t - win // 2 + win, T) - jnp.maximum(t - win // 2, 0)).astype(F32)
                cg = jnp.broadcast_to(cg, (n, C))
                cnt = cg if cnt is None else jnp.where(lane >= g * gdim, cg, cnt)
            mean = sel / cnt
        else:
            mean = sel * inv_win
        mixed = mean - w[HALO:HALO + n]
        y = jnp.dot(mixed.astype(BF16), wp_ref[...], preferred_element_type=F32) * ps_ref[...]
        p_out_ref[0, pl.ds(r0, n), :] = y.astype(p_out_ref.dtype)

    nv = tc // SUBLANES

    def fwd(c, h):
        r0 = pl.multiple_of(c * tc, tc)
        xc = conv_chunk(r0)
        xc_ref[pl.ds(r0, tc), :] = xc
        a, b = coeffs(xc, 0)
        a3, b3 = _scan8(a, b, reverse=False)
        for j in range(nv):
            hj = b3[j] + a3[j] * h
            hf_ref[pl.ds(r0 + j * SUBLANES, SUBLANES), :] = hj
            h = jnp.broadcast_to(hj[SUBLANES - 1:SUBLANES], (SUBLANES, C))
        return h

    unroll = 2 if nc % 2 == 0 else 1
    h = lax.fori_loop(0, nc, fwd, jnp.broadcast_to(h0_ref[0, 0:1, :], (SUBLANES, C)), unroll=unroll)
    hfin_ref[0, 0:1, :] = h[0:1]

    def bwd(cc, h):
        c = nc - 1 - cc
        r0 = pl.multiple_of(c * tc, tc)
        pool_rows(r0, tc, False)
        a, b = coeffs(xc_ref[pl.ds(r0, tc), :], 1)
        a3, b3 = _scan8(a, b, reverse=True)
        for j in range(nv - 1, -1, -1):
            hj = b3[j] + a3[j] * h
            rows = pl.ds(r0 + j * SUBLANES, SUBLANES)
            y = hj + hf_ref[rows, :]
            a_out_ref[0, rows, :] = (y * jax.nn.gelu(lg_ref[0, rows, :])).astype(a_out_ref.dtype)
            h = jnp.broadcast_to(hj[0:1], (SUBLANES, C))
        return h

    h = lax.fori_loop(0, nc, bwd, jnp.broadcast_to(h0_ref[0, 1:2, :], (SUBLANES, C)), unroll=unroll)
    hfin_ref[0, 1:2, :] = h[0:1]

    pool_rows(0, BF16_ROWS, True)
    pool_rows(T - BF16_ROWS, BF16_ROWS, True)


def _seq_call(lx, lg, pu, h0, lw, tc):
    B, T, C = lx.shape
    seq = pl.BlockSpec((1, T, C), lambda b: (b, 0, 0))
    st = pl.BlockSpec((1, 2, C), lambda b: (b, 0, 0))
    return pl.pallas_call(
        functools.partial(_seq_kernel, tc=tc),
        grid=(B,),
        in_specs=[seq, seq, seq, st, _const_spec((CONV_W, C)), _const_spec((1, C)),
                  _const_spec((C, 4 * C)), _const_spec((1, 4 * C)), _const_spec((2, C)),
                  _const_spec((C, C)), _const_spec((1, C))],
        out_specs=[seq, seq, st],
        out_shape=[jax.ShapeDtypeStruct((B, T, C), BF16), jax.ShapeDtypeStruct((B, T, C), BF16),
                   jax.ShapeDtypeStruct((B, 2, C), F32)],
        scratch_shapes=[pltpu.VMEM((T + 2 * HALO, C), F32), pltpu.VMEM((T + 2 * HALO, C), F32),
                        pltpu.VMEM((T, C), F32), pltpu.VMEM((T, C), F32)],
        compiler_params=_cparams(1),
        name="seq_mix",
    )(lx, lg, pu, h0, lw["conv_w"], lw["conv_b"], lw["w_gate"], lw["b_gate"], lw["lam"],
      lw["w_pool"], lw["pool_scale"])


def _attn_kernel(*refs, n_seg, tq_sub, kc):
    q_ref = refs[0]
    kv_refs = refs[1:1 + 2 * n_seg]
    o_ref = refs[1 + 2 * n_seg]
    s_scr = refs[2 + 2 * n_seg]
    n_sub = q_ref.shape[2] // tq_sub
    units = [(hh, j) for j in range(n_sub) for hh in range(2)]
    chunks = []
    off = 0
    for seg in range(n_seg):
        lk = kv_refs[2 * seg].shape[2]
        for st in range(0, lk, kc):
            sz = min(kc, lk - st)
            chunks.append((seg, st, sz, off))
            off += sz
    nch = len(chunks)
    mvec = {}
    col_max = {}
    acc = {}
    results = {}
    pv_hist = []
    depth = 5

    def score_chunk(u, c):
        hh, j = units[u]
        seg, st, sz, off = chunks[c]
        q = q_ref[0, hh, j * tq_sub:(j + 1) * tq_sub, :]
        k = kv_refs[2 * seg][0, hh, st:st + sz, :]
        s = lax.dot_general(k, q, (((1,), (1,)), ((), ())), preferred_element_type=F32)
        s_scr[u % 2, off:off + sz, :] = s
        cm = jnp.max(s.reshape(sz // SUBLANES, SUBLANES, tq_sub), axis=0)
        mvec[u] = cm if c == 0 else jnp.maximum(mvec[u], cm)
        if c == nch - 1:
            col_max[u] = jnp.max(mvec[u], axis=0, keepdims=True)

    def weight_chunk(u, c):
        hh, j = units[u]
        seg, st, sz, off = chunks[c]
        m = col_max[u]
        if len(pv_hist) >= depth:
            bits = pltpu.bitcast(pv_hist[-depth][MLA_V:MLA_V + 1], jnp.uint32)
            m = m + ((bits >> 16) >> 16).astype(F32)
        p = jnp.exp2((s_scr[u % 2, off:off + sz, :] - m).astype(BF16))
        vt = kv_refs[2 * seg + 1][0, 0, hh * V_ROWS:(hh + 1) * V_ROWS, st:st + sz]
        pv = jnp.dot(vt, p, preferred_element_type=F32)
        pv_hist.append(pv)
        acc[u] = pv if c == 0 else acc[u] + pv
        if c == nch - 1:
            results[units[u]] = acc[u][:MLA_V] / acc[u][MLA_V:MLA_V + 1]

    lead = nch + min(4, nch - 1)
    events = []
    for u in range(len(units)):
        for c in range(nch):
            events.append((u * nch + c - lead, 0, u, c))
            events.append((u * nch + c, 1, u, c))
    for _, kind, u, c in sorted(events):
        (weight_chunk if kind else score_chunk)(u, c)

    for j in range(n_sub):
        o_t = jnp.concatenate([results[(0, j)], results[(1, j)]], axis=0)
        o_ref[0, j * tq_sub:(j + 1) * tq_sub, :] = o_t.T.astype(o_ref.dtype)


def _attn_call(q, segs, tq):
    B, H, T, _ = q.shape
    tq_sub = min(tq, 256)
    in_specs = [pl.BlockSpec((1, 2, tq, LANES), lambda b, p, i: (b, p, i, 0))]
    args = [q]
    lk_total = 0
    for k, v in segs:
        lk = k.shape[2]
        lk_total += lk
        in_specs.append(pl.BlockSpec((1, 2, lk, LANES), lambda b, p, i: (b, p, 0, 0)))
        in_specs.append(pl.BlockSpec((1, 1, 2 * V_ROWS, lk), lambda b, p, i: (b, p, 0, 0)))
        args += [k, v]
    return pl.pallas_call(
        functools.partial(_attn_kernel, n_seg=len(segs), tq_sub=tq_sub, kc=256),
        grid=(B, H // 2, T // tq),
        in_specs=in_specs,
        out_specs=pl.BlockSpec((1, tq, LANES), lambda b, p, i: (b, i, p)),
        out_shape=jax.ShapeDtypeStruct((B, T, H * MLA_V), BF16),
        scratch_shapes=[pltpu.VMEM((2, lk_total, tq_sub), F32)],
        compiler_params=_cparams(3),
        name="attention",
    )(*args)


def _post_kernel(x_ref, a_ref, att_ref, p_ref, g1_ref, sh_ref, sc_ref, g2_ref, gn_ref, wo_ref,
                 w1_ref, w2_ref, o_ref, *, ff_chunk):
    d_lru = a_ref.shape[2]
    d_att = att_ref.shape[2]
    y = jnp.dot(a_ref[0], wo_ref[0:d_lru, :], preferred_element_type=F32)
    y += jnp.dot(att_ref[0], wo_ref[d_lru:d_lru + d_att, :], preferred_element_type=F32)
    y += jnp.dot(p_ref[0], wo_ref[d_lru + d_att:, :], preferred_element_type=F32)
    x1 = x_ref[0] + g1_ref[0] * y
    xn = x1 * lax.rsqrt(jnp.mean(x1 * x1, axis=-1, keepdims=True) + EPS)
    h = ((xn * gn_ref[...]) * (1.0 + sc_ref[0]) + sh_ref[0]).astype(BF16)
    d_ff = w1_ref.shape[1]
    acc = None
    for c in range(d_ff // ff_chunk):
        u = jnp.dot(h, w1_ref[:, c * ff_chunk:(c + 1) * ff_chunk], preferred_element_type=F32)
        u = jnp.square(jnp.maximum(u, 0.0)).astype(BF16)
        part = jnp.dot(u, w2_ref[c * ff_chunk:(c + 1) * ff_chunk, :], preferred_element_type=F32)
        acc = part if acc is None else acc + part
    o_ref[0] = x1 + g2_ref[0] * acc


def _post_call(x, a, att, p, g1, sh, sc, g2, gn, lw, tm):
    B, T, D = x.shape
    tok = lambda w: pl.BlockSpec((1, tm, w), lambda b, i: (b, i, 0))
    vec = pl.BlockSpec((1, 1, D), lambda b, i: (b, 0, 0))
    single = lambda shape: pl.BlockSpec(shape, lambda b, i: (0,) * len(shape),
                                        pipeline_mode=pl.Buffered(1))
    return pl.pallas_call(
        functools.partial(_post_kernel, ff_chunk=1024),
        grid=(B, T // tm),
        in_specs=[tok(D), tok(a.shape[2]), tok(att.shape[2]), tok(p.shape[2]), vec, vec, vec, vec,
                  _const_spec((1, D)), single(lw["w_out"].shape), single(lw["w_ff1"].shape),
                  single(lw["w_ff2"].shape)],
        out_specs=tok(D),
        out_shape=jax.ShapeDtypeStruct((B, T, D), F32),
        compiler_params=_cparams(2),
        name="out_proj_mlp",
    )(x, a, att, p, g1, sh, sc, g2, gn, lw["w_out"], lw["w_ff1"], lw["w_ff2"])


def _rope_partner():
    j = np.arange(MLA_ROPE)
    half = (j % (2 * ROPE_FREQS)) // ROPE_FREQS
    return np.where(half == 0, j + ROPE_FREQS, j - ROPE_FREQS)


def _block_diag(w):
    n, a, b = w.shape
    eye = jnp.asarray(np.eye(n, dtype=np.float32), w.dtype)
    return (w[:, :, None, :] * eye[:, None, :, None]).reshape(n * a, n * b)


def _layer_weights(l, w_in, conv_w, conv_b, lru_w_a, lru_b_a, lru_w_x, lru_b_x, lru_lambda, g_q_lat,
                   w_uq, g_kv_lat, w_ukv, g_qn, g_kn, w_pool, pool_scale, w_out, w_ff1, w_ff2):
    d = w_in.shape[1]
    d_lru = conv_w.shape[2]
    q_rank = w_uq.shape[1]
    kv_rank = w_ukv.shape[1]
    heads = w_uq.shape[2] // MLA_QK
    partner = _rope_partner()
    o_kr = 2 * d_lru + q_rank + kv_rank
    wi = w_in[l]
    kr = wi[:, o_kr:o_kr + MLA_ROPE]
    krb = jnp.concatenate([jnp.zeros((d, MLA_NOPE), F32), kr, kr[:, partner]], axis=1)
    w_in_p = jnp.concatenate([wi[:, :o_kr], krb, wi[:, o_kr + MLA_ROPE:]], axis=1).astype(BF16)
    wq = w_uq[l].reshape(q_rank, heads, MLA_QK)
    pad = jnp.zeros((q_rank, heads, HEAD_PAD - MLA_QK), F32)
    wq_main = jnp.concatenate([wq, pad], axis=2)
    wq_partner = wq[:, :, MLA_NOPE + partner]
    wq = jnp.concatenate([wq_main.reshape(q_rank, heads * HEAD_PAD),
                          wq_partner.reshape(q_rank, heads * MLA_ROPE)], axis=1)
    wkv = w_ukv[l].reshape(kv_rank, heads, MLA_NOPE + MLA_V)
    wk = jnp.concatenate([wkv[:, :, :MLA_NOPE], jnp.zeros((kv_rank, heads, HEAD_PAD - MLA_NOPE), F32)],
                         axis=2).reshape(kv_rank, heads * HEAD_PAD)
    wv = wkv[:, :, MLA_NOPE:].reshape(kv_rank, heads * MLA_V)
    zpad = jnp.zeros((HEAD_PAD - MLA_QK,), F32)
    znope = jnp.zeros((MLA_NOPE,), F32)
    q_scale = LOG2E / math.sqrt(MLA_QK)
    gq = jnp.concatenate([g_qn[l], zpad]) * q_scale
    gq_sw = jnp.concatenate([znope, g_qn[l][MLA_NOPE + partner], zpad]) * q_scale
    gk = jnp.concatenate([g_kn[l], zpad])
    gk_sw = jnp.concatenate([znope, g_kn[l][MLA_NOPE + partner], zpad])
    head_ones = jnp.asarray(np.kron(np.eye(2), np.ones((HEAD_PAD, HEAD_PAD))), BF16)
    w_gate = jnp.concatenate([_block_diag(lru_w_a[l, 0]), _block_diag(lru_w_x[l, 0]),
                              _block_diag(lru_w_a[l, 1]), _block_diag(lru_w_x[l, 1])], axis=1)
    b_gate = jnp.concatenate([lru_b_a[l, 0], lru_b_x[l, 0], lru_b_a[l, 1], lru_b_x[l, 1]])
    return dict(
        heads=heads, d_lru=d_lru, q_rank=q_rank, kv_rank=kv_rank,
        w_in=w_in_p, g_q_lat=g_q_lat[l][None], w_uq=wq.astype(BF16), g_kv_lat=g_kv_lat[l][None],
        w_k=wk.astype(BF16), w_v=wv.astype(BF16), head_gains=jnp.stack([gq, gq_sw, gk, gk_sw]),
        head_ones=head_ones,
        conv_w=conv_w[l], conv_b=conv_b[l][None], w_gate=w_gate.astype(BF16), b_gate=b_gate[None],
        lam=lru_lambda[l], w_pool=_block_diag(w_pool[l]).astype(BF16), pool_scale=pool_scale[l][None],
        w_out=w_out[l].astype(BF16), w_ff1=w_ff1[l].astype(BF16), w_ff2=w_ff2[l].astype(BF16))


def _rope_tables(T):
    f32 = np.float32
    j = np.arange(MLA_ROPE)
    axis = j // (2 * ROPE_FREQS)
    half = (j % (2 * ROPE_FREQS)) // ROPE_FREQS
    f = j % ROPE_FREQS
    t = np.arange(T)
    pos = np.stack([(t // GRID_W).astype(f32), (t % GRID_W).astype(f32)], axis=-1)
    freqs = np.power(f32(ROPE_BASE), -np.arange(ROPE_FREQS, dtype=f32) / f32(ROPE_FREQS)).astype(f32)
    ang = (pos[:, axis] * freqs[f]).astype(f32)
    sign = np.where(half == 0, -1.0, 1.0).astype(f32)
    ct = np.concatenate([np.ones((T, MLA_NOPE), f32), np.cos(ang).astype(f32),
                         np.zeros((T, HEAD_PAD - MLA_QK), f32)], axis=1)
    st = np.concatenate([np.zeros((T, MLA_NOPE), f32), (np.sin(ang) * sign).astype(f32),
                         np.zeros((T, HEAD_PAD - MLA_QK), f32)], axis=1)
    return jnp.asarray(ct), jnp.asarray(st)


def _no_rope_tables(T):
    lane = np.arange(HEAD_PAD)
    ct = np.broadcast_to((lane < MLA_QK).astype(np.float32), (T, HEAD_PAD))
    return jnp.asarray(ct), jnp.zeros((T, HEAD_PAD), F32)


def _tile(n, pref):
    t = min(n, pref)
    while n % t:
        t //= 2
    return t


def kernel(x, c, ctx, c_ctx, w_mod, b_mod, g_norm1, g_norm2, w_in, conv_w, conv_b, lru_w_a, lru_b_a,
           lru_w_x, lru_b_x, lru_lambda, g_q_lat, w_uq, g_kv_lat, w_ukv, g_qn, g_kn, w_pool, pool_scale,
           w_out, w_ff1, w_ff2):
    B, L, D = x.shape
    Lc = ctx.shape[1]
    depth = w_mod.shape[0]
    d_lru = conv_w.shape[2]

    rows = -(-(B + 1) // SUBLANES) * SUBLANES
    cvec = jnp.concatenate([c, c_ctx[None], jnp.zeros((rows - B - 1, D), F32)], axis=0)
    mods = _mod_call(cvec, w_mod, b_mod)

    rope_l = _rope_tables(L)
    rope_c = _no_rope_tables(Lc)
    tm_l, tm_c = _tile(L, 1024), _tile(Lc, 1024)
    tp_l, tp_c = _tile(L, 1024), _tile(Lc, 1024)
    tq_l, tq_c = _tile(L, 2048), _tile(Lc, 2048)
    tc_l, tc_c = _tile(L, 256), _tile(Lc, 256)

    h = ctx
    zeros_state = jnp.zeros((B, 2, d_lru), F32)
    for l in range(depth):
        last = l == depth - 1
        lw = _layer_weights(l, w_in, conv_w, conv_b, lru_w_a, lru_b_a, lru_w_x, lru_b_x, lru_lambda,
                            g_q_lat, w_uq, g_kv_lat, w_ukv, g_qn, g_kn, w_pool, pool_scale, w_out,
                            w_ff1, w_ff2)
        ml = mods[l, :B].reshape(B, 1, 6, D)
        mc = jnp.broadcast_to(mods[l, B].reshape(1, 1, 6, D), (B, 1, 6, D))
        sh1, sc1, g1, sh2, sc2, g2 = (ml[:, :, i] for i in range(6))
        csh1, csc1, cg1, csh2, csc2, cg2 = (mc[:, :, i] for i in range(6))
        gn1 = g_norm1[l][None]
        gn2 = g_norm2[l][None]

        lx_c, lg_c, pu_c, q_c, k_c, v_c = _inproj_call(h, csh1, csc1, gn1, lw, rope_c, tm_c)
        a_c, p_c, hfin = _seq_call(lx_c, lg_c, pu_c, zeros_state, lw, tc_c)
        lx_l, lg_l, pu_l, q_l, k_l, v_l = _inproj_call(x, sh1, sc1, gn1, lw, rope_l, tm_l)
        a_l, p_l, _ = _seq_call(lx_l, lg_l, pu_l, hfin, lw, tc_l)
        att_l = _attn_call(q_l, [(k_c, v_c), (k_l, v_l)], tq_l)
        x = _post_call(x, a_l, att_l, p_l, g1, sh2, sc2, g2, gn2, lw, tp_l)
        if not last:
            att_c = _attn_call(q_c, [(k_c, v_c)], tq_c)
            h = _post_call(h, a_c, att_c, p_c, cg1, csh2, csc2, cg2, gn2, lw, tp_c)
    return x
```

```python
import functools
import math

import jax
import jax.numpy as jnp
import numpy as np
from jax import lax
from jax.experimental import pallas as pl
from jax.experimental.pallas import tpu as pltpu

F32 = jnp.float32
BF16 = jnp.bfloat16

EPS = 1e-6
GRID_W = 64
LRU_HEADS = 4
LRU_C = 8.0
CONV_W = 4
MLA_V = 64
MLA_NOPE = 64
MLA_ROPE = 32
MLA_QK = MLA_NOPE + MLA_ROPE
ROPE_FREQS = MLA_ROPE // 4
ROPE_BASE = 10000.0
POOL_WINDOWS = (2, 4, 8, 16)

LANES = 128
SUBLANES = 8
HEAD_PAD = LANES
BF16_ROWS = 16
V_ROWS = MLA_V + BF16_ROWS
HALO = 16
SCAN_GROUP = 4
SCAN_BLOCK = SCAN_GROUP * SUBLANES
VMEM_LIMIT = 56 * 1024 * 1024

LOG2E = 1.4426950408889634


def _cparams(n_grid, flags=None):
    return pltpu.CompilerParams(dimension_semantics=("arbitrary",) * n_grid,
                                vmem_limit_bytes=VMEM_LIMIT, flags=flags)


def _const_spec(shape):
    zeros = (0,) * len(shape)
    return pl.BlockSpec(shape, lambda *_: zeros)


def _mod_kernel(c_ref, w_ref, b_ref, o_ref):
    c = c_ref[...]
    act = (c * jax.nn.sigmoid(c)).astype(BF16)
    o_ref[0] = jnp.dot(act, w_ref[0].astype(BF16), preferred_element_type=F32) + b_ref[0]


def _mod_call(cvec, w_mod, b_mod):
    depth, d, n = w_mod.shape
    rows = cvec.shape[0]
    tn = 1024
    return pl.pallas_call(
        _mod_kernel,
        grid=(depth, n // tn),
        in_specs=[pl.BlockSpec((rows, d), lambda l, j: (0, 0)),
                  pl.BlockSpec((1, d, tn), lambda l, j: (l, 0, j)),
                  pl.BlockSpec((1, 1, tn), lambda l, j: (l, 0, j))],
        out_specs=pl.BlockSpec((1, rows, tn), lambda l, j: (l, 0, j)),
        out_shape=jax.ShapeDtypeStruct((depth, rows, n), F32),
        compiler_params=_cparams(2),
        name="adaln_mod",
    )(cvec, w_mod, b_mod.reshape(depth, 1, n))


def _inproj_kernel(x_ref, sh_ref, sc_ref, g1_ref, win_ref, gql_ref, wuq_ref, gkvl_ref, wk_ref,
                   wv_ref, ones_ref, gains_ref, ct_ref, st_ref,
                   lx_ref, lg_ref, pu_ref, q_ref, k_ref, v_ref, *, d_lru, q_rank, kv_rank, heads, rsub):
    lane = lax.broadcasted_iota(jnp.int32, (1, HEAD_PAD), 1)
    rot = (lane >= MLA_NOPE) & (lane < MLA_QK)
    gains = gains_ref[...]
    ones = ones_ref[...]
    inv_qk = 1.0 / MLA_QK
    n_main = heads * HEAD_PAD
    pair = 2 * HEAD_PAD
    extra = (lax.broadcasted_iota(jnp.int32, (BF16_ROWS, rsub), 0) == 0).astype(F32)

    for r0 in range(0, x_ref.shape[1], rsub):
        rows = slice(r0, r0 + rsub)
        x = x_ref[0, rows, :]
        xn = x * lax.rsqrt(jnp.mean(x * x, axis=-1, keepdims=True) + EPS)
        hx = (xn * g1_ref[...]) * (1.0 + sc_ref[0]) + sh_ref[0]
        z = jnp.dot(hx.astype(BF16), win_ref[...], preferred_element_type=F32)
        o = 0
        lx_ref[0, rows, :] = z[:, o:o + d_lru]; o += d_lru
        lg_ref[0, rows, :] = z[:, o:o + d_lru]; o += d_lru
        ql = z[:, o:o + q_rank]; o += q_rank
        kvl = z[:, o:o + kv_rank]; o += kv_rank
        krb = z[:, o:o + HEAD_PAD]; o += HEAD_PAD
        pu_ref[0, rows, :] = z[:, o:]

        qln = ql * lax.rsqrt(jnp.mean(ql * ql, axis=-1, keepdims=True) + EPS) * gql_ref[...]
        qf = jnp.dot(qln.astype(BF16), wuq_ref[...], preferred_element_type=F32)
        kvn = kvl * lax.rsqrt(jnp.mean(kvl * kvl, axis=-1, keepdims=True) + EPS) * gkvl_ref[...]
        kvn = kvn.astype(BF16)
        kf = jnp.dot(kvn, wk_ref[...], preferred_element_type=F32)
        vf = jnp.dot(kvn, wv_ref[...], preferred_element_type=F32)

        kr = jnp.where(rot, krb, 0.0)
        kr_sw = pltpu.roll(krb, HEAD_PAD - MLA_ROPE, axis=1)
        kr2 = jnp.concatenate([kr, kr], axis=1)
        ct, st = ct_ref[rows, :], st_ref[rows, :]
        aq, bq, ak, bk = ct * gains[0:1], st * gains[1:2], ct * gains[2:3], st * gains[3:4]
        for hp in range(heads // 2):
            qa = qf[:, hp * pair:(hp + 1) * pair]
            ss = jnp.dot((qa * qa).astype(BF16), ones, preferred_element_type=F32)
            rq = lax.rsqrt(ss * inv_qk + EPS)
            ka = kf[:, hp * pair:(hp + 1) * pair] + kr2
            ss = jnp.dot((ka * ka).astype(BF16), ones, preferred_element_type=F32)
            rk = lax.rsqrt(ss * inv_qk + EPS)
            for e in range(2):
                h = 2 * hp + e
                lo, hi = e * HEAD_PAD, (e + 1) * HEAD_PAD
                grp = qf[:, n_main + (h // 4) * HEAD_PAD:n_main + (h // 4 + 1) * HEAD_PAD]
                shift = (MLA_NOPE - (h % 4) * MLA_ROPE) % HEAD_PAD
                qb = pltpu.roll(grp, shift, axis=1) if shift else grp
                q_ref[0, h, rows, :] = (rq[:, lo:hi] * (qa[:, lo:hi] * aq + qb * bq)).astype(BF16)
                k_ref[0, h, rows, :] = (rk[:, lo:hi] * (ka[:, lo:hi] * ak + kr_sw * bk)).astype(BF16)
        for p in range(heads // 2):
            vt = vf[:, p * LANES:(p + 1) * LANES].T
            v_ref[0, p, :, rows] = jnp.concatenate(
                [vt[:MLA_V], extra, vt[MLA_V:], extra], axis=0).astype(BF16)


def _inproj_call(x, sh, sc, g1, lw, tables, tm):
    B, T, D = x.shape
    pos = pl.BlockSpec((tm, HEAD_PAD), lambda b, i: (i, 0))
    heads = lw["heads"]
    d_lru, q_rank, kv_rank = lw["d_lru"], lw["q_rank"], lw["kv_rank"]
    d_pool = lw["w_in"].shape[1] - (2 * d_lru + q_rank + kv_rank + HEAD_PAD)
    tok = lambda w: pl.BlockSpec((1, tm, w), lambda b, i: (b, i, 0))
    vec = pl.BlockSpec((1, 1, D), lambda b, i: (b, 0, 0))
    hd = lambda n: pl.BlockSpec((1, n, tm, LANES), lambda b, i: (b, 0, i, 0))
    kern = functools.partial(_inproj_kernel, d_lru=d_lru, q_rank=q_rank, kv_rank=kv_rank, heads=heads,
                             rsub=min(tm, 256))
    return pl.pallas_call(
        kern,
        grid=(B, T // tm),
        in_specs=[tok(D), vec, vec, _const_spec((1, D)), _const_spec(lw["w_in"].shape),
                  _const_spec((1, q_rank)), _const_spec(lw["w_uq"].shape),
                  _const_spec((1, kv_rank)), _const_spec(lw["w_k"].shape), _const_spec(lw["w_v"].shape),
                  _const_spec(lw["head_ones"].shape), _const_spec(lw["head_gains"].shape), pos, pos],
        out_specs=[tok(d_lru), tok(d_lru), tok(d_pool), hd(heads), hd(heads),
                   pl.BlockSpec((1, heads // 2, 2 * V_ROWS, tm), lambda b, i: (b, 0, 0, i))],
        out_shape=[jax.ShapeDtypeStruct((B, T, d_lru), F32),
                   jax.ShapeDtypeStruct((B, T, d_lru), F32),
                   jax.ShapeDtypeStruct((B, T, d_pool), F32),
                   jax.ShapeDtypeStruct((B, heads, T, LANES), BF16),
                   jax.ShapeDtypeStruct((B, heads, T, LANES), BF16),
                   jax.ShapeDtypeStruct((B, heads // 2, 2 * V_ROWS, T), BF16)],
        compiler_params=_cparams(2),
        name="in_proj",
    )(x, sh, sc, g1, lw["w_in"], lw["g_q_lat"], lw["w_uq"], lw["g_kv_lat"], lw["w_k"], lw["w_v"],
      lw["head_ones"], lw["head_gains"], *tables)


def _roll_rows(a, shift):
    return pltpu.roll(a, shift % a.shape[0], axis=0)


def _scan8(a, b, reverse):
    n = a.shape[0]
    a3 = a.reshape(n // SUBLANES, SUBLANES, a.shape[1])
    b3 = b.reshape(n // SUBLANES, SUBLANES, b.shape[1])
    rid = lax.broadcasted_iota(jnp.int32, a3.shape, 1)
    for s in (1, 2, 4):
        if reverse:
            m = rid < SUBLANES - s
            sh = SUBLANES - s
        else:
            m = rid >= s
            sh = s
        a_sh = pltpu.roll(a3, sh, axis=1)
        b_sh = pltpu.roll(b3, sh, axis=1)
        b3 = jnp.where(m, a3 * b_sh + b3, b3)
        a3 = jnp.where(m, a3 * a_sh, a3)
    return a3, b3


def _chunk_scan(a, b, h, sa_ref, sb_ref, out_ref, row0, reverse):
    tc, C = a.shape
    rid = lax.broadcasted_iota(jnp.int32, (SUBLANES, LANES), 0)
    order = list(range(SCAN_GROUP - 1, -1, -1) if reverse else range(SCAN_GROUP))
    n_blocks = tc // SCAN_BLOCK
    blocks = range(n_blocks - 1, -1, -1) if reverse else range(n_blocks)
    for s in range(C // LANES):
        sa_ref[s] = a[:, s * LANES:(s + 1) * LANES]
        sb_ref[s] = b[:, s * LANES:(s + 1) * LANES]
    h_out = []
    for s in range(C // LANES):
        hs = h[:, s * LANES:(s + 1) * LANES]
        for k in blocks:
            base = k * SCAN_BLOCK
            av = [sa_ref[s, pl.ds(base + j, SUBLANES, stride=SCAN_GROUP), :] for j in range(SCAN_GROUP)]
            bv = [sb_ref[s, pl.ds(base + j, SUBLANES, stride=SCAN_GROUP), :] for j in range(SCAN_GROUP)]
            hh, pp = {}, {}
            prev = None
            for j in order:
                if prev is None:
                    hh[j], pp[j] = bv[j], av[j]
                else:
                    hh[j], pp[j] = av[j] * hh[prev] + bv[j], av[j] * pp[prev]
                prev = j
            pc, hc = _scan8(pp[prev], hh[prev], reverse)
            end = hc.reshape(SUBLANES, LANES) + pc.reshape(SUBLANES, LANES) * hs
            if reverse:
                enter = jnp.where(rid == SUBLANES - 1, hs, pltpu.roll(end, SUBLANES - 1, axis=0))
                edge = end[0:1]
            else:
                enter = jnp.where(rid == 0, hs, pltpu.roll(end, 1, axis=0))
                edge = end[SUBLANES - 1:SUBLANES]
            for j in range(SCAN_GROUP):
                out_ref[s, pl.ds(row0 + base + j, SUBLANES, stride=SCAN_GROUP), :] = hh[j] + pp[j] * enter
            hs = jnp.broadcast_to(edge, (SUBLANES, LANES))
        h_out.append(hs)
    return jnp.concatenate(h_out, axis=1)


def _seq_kernel(lx_ref, lg_ref, pu_ref, h0_ref, cw_ref, cb_ref, wg_ref, bg_ref, lam_ref, wp_ref,
                ps_ref, a_out_ref, p_out_ref, hfin_ref, xpad_ref, upad_ref, hf_ref, xc_ref,
                sa_ref, sb_ref, hb_ref, *, tc):
    T = lx_ref.shape[1]
    C = lx_ref.shape[2]
    nc = T // tc
    zeros_halo = jnp.zeros((HALO, C), F32)
    xpad_ref[0:HALO, :] = zeros_halo
    xpad_ref[HALO + T:HALO + T + HALO, :] = zeros_halo
    upad_ref[0:HALO, :] = zeros_halo
    upad_ref[HALO + T:HALO + T + HALO, :] = zeros_halo

    def fill(c, carry):
        r0 = pl.multiple_of(c * tc, tc)
        xpad_ref[pl.ds(HALO + r0, tc), :] = lx_ref[0, pl.ds(r0, tc), :]
        upad_ref[pl.ds(HALO + r0, tc), :] = pu_ref[0, pl.ds(r0, tc), :]
        return carry

    lax.fori_loop(0, nc, fill, 0)

    lam = lam_ref[...]
    coef = -LRU_C * (jnp.maximum(-lam, 0.0) + jnp.log(1.0 + jnp.exp(-jnp.abs(lam))))
    cw = cw_ref[...]
    cb = cb_ref[...]
    bg = bg_ref[...]

    def conv_chunk(r0):
        w = xpad_ref[pl.ds(r0 + HALO - SUBLANES, tc + 2 * SUBLANES), :]
        y = (cw[0:1] * _roll_rows(w, 1) + cw[1:2] * w + cw[2:3] * _roll_rows(w, -1)
             + cw[3:4] * _roll_rows(w, -2))
        return y[SUBLANES:SUBLANES + tc] + cb

    def coeffs(xc, d):
        g = jnp.dot(xc.astype(BF16), wg_ref[:, 2 * C * d:2 * C * (d + 1)],
                    preferred_element_type=F32) + bg[:, 2 * C * d:2 * C * (d + 1)]
        r = jax.nn.sigmoid(g[:, :C])
        i = jax.nn.sigmoid(g[:, C:])
        log_a = coef[d:d + 1] * r
        a = jnp.exp(log_a)
        b = jnp.exp2(0.5 * jnp.log2(1.0 - a * a)) * (i * xc)
        return a, b

    lane = lax.broadcasted_iota(jnp.int32, (1, C), 1)
    gdim = C // len(POOL_WINDOWS)

    inv_win = None
    for g, win in enumerate(POOL_WINDOWS):
        inv_g = jnp.full((1, C), 1.0 / win, F32)
        inv_win = inv_g if inv_win is None else jnp.where(lane >= g * gdim, inv_g, inv_win)

    def pool_rows(r0, n, edge):
        w = upad_ref[pl.ds(r0, n + 2 * HALO), :]
        s2 = w + _roll_rows(w, 1)
        s4 = _roll_rows(s2, 1) + _roll_rows(s2, -1)
        s8 = _roll_rows(s4, 2) + _roll_rows(s4, -2)
        s16 = _roll_rows(s8, 4) + _roll_rows(s8, -4)
        sel = s2[HALO:HALO + n]
        for g, s in enumerate((s4, s8, s16), start=1):
            sel = jnp.where(lane >= g * gdim, s[HALO:HALO + n], sel)
        if edge:
            t = r0 + lax.broadcasted_iota(jnp.int32, (n, 1), 0)
            cnt = None
            for g, win in enumerate(POOL_WINDOWS):
                cg = (jnp.minimum(t - win // 2 + win, T) - jnp.maximum(t - win // 2, 0)).astype(F32)
                cg = jnp.broadcast_to(cg, (n, C))
                cnt = cg if cnt is None else jnp.where(lane >= g * gdim, cg, cnt)
            mean = sel / cnt
        else:
            mean = sel * inv_win
        mixed = mean - w[HALO:HALO + n]
        y = jnp.dot(mixed.astype(BF16), wp_ref[...], preferred_element_type=F32) * ps_ref[...]
        p_out_ref[0, pl.ds(r0, n), :] = y.astype(p_out_ref.dtype)

    per = 2 if nc % 2 == 0 else 1
    n_slab = C // LANES

    def fwd(i, h):
        for par in range(per):
            r0 = pl.multiple_of((per * i + par) * tc, tc)
            xc = conv_chunk(r0)
            xc_ref[pl.ds(r0, tc), :] = xc
            a, b = coeffs(xc, 0)
            h = _chunk_scan(a, b, h, sa_ref.at[par], sb_ref.at[par], hf_ref, r0, False)
        return h

    h = lax.fori_loop(0, nc // per, fwd, jnp.broadcast_to(h0_ref[0, 0:1, :], (SUBLANES, C)))
    hfin_ref[0, 0:1, :] = h[0:1]

    def bwd(i, h):
        for par in range(per):
            r0 = pl.multiple_of((nc - 1 - (per * i + par)) * tc, tc)
            rows = pl.ds(r0, tc)
            pool_rows(r0, tc, False)
            a, b = coeffs(xc_ref[rows, :], 1)
            h = _chunk_scan(a, b, h, sa_ref.at[par], sb_ref.at[par], hb_ref.at[par], 0, True)
            y = (jnp.concatenate([hb_ref[par, s] for s in range(n_slab)], axis=1)
                 + jnp.concatenate([hf_ref[s, rows, :] for s in range(n_slab)], axis=1))
            a_out_ref[0, rows, :] = (y * jax.nn.gelu(lg_ref[0, rows, :])).astype(a_out_ref.dtype)
        return h

    h = lax.fori_loop(0, nc // per, bwd, jnp.broadcast_to(h0_ref[0, 1:2, :], (SUBLANES, C)))
    hfin_ref[0, 1:2, :] = h[0:1]

    pool_rows(0, BF16_ROWS, True)
    pool_rows(T - BF16_ROWS, BF16_ROWS, True)


def _seq_call(lx, lg, pu, h0, lw, tc):
    B, T, C = lx.shape
    seq = pl.BlockSpec((1, T, C), lambda b: (b, 0, 0))
    st = pl.BlockSpec((1, 2, C), lambda b: (b, 0, 0))
    return pl.pallas_call(
        functools.partial(_seq_kernel, tc=tc),
        grid=(B,),
        in_specs=[seq, seq, seq, st, _const_spec((CONV_W, C)), _const_spec((1, C)),
                  _const_spec((C, 4 * C)), _const_spec((1, 4 * C)), _const_spec((2, C)),
                  _const_spec((C, C)), _const_spec((1, C))],
        out_specs=[seq, seq, st],
        out_shape=[jax.ShapeDtypeStruct((B, T, C), BF16), jax.ShapeDtypeStruct((B, T, C), BF16),
                   jax.ShapeDtypeStruct((B, 2, C), F32)],
        scratch_shapes=[pltpu.VMEM((T + 2 * HALO, C), F32), pltpu.VMEM((T + 2 * HALO, C), F32),
                        pltpu.VMEM((C // LANES, T, LANES), F32), pltpu.VMEM((T, C), F32),
                        pltpu.VMEM((2, C // LANES, tc, LANES), F32),
                        pltpu.VMEM((2, C // LANES, tc, LANES), F32),
                        pltpu.VMEM((2, C // LANES, tc, LANES), F32)],
        compiler_params=_cparams(1),
        name="seq_mix",
    )(lx, lg, pu, h0, lw["conv_w"], lw["conv_b"], lw["w_gate"], lw["b_gate"], lw["lam"],
      lw["w_pool"], lw["pool_scale"])


def _attn_kernel(*refs, n_seg, tq_sub, kc):
    q_ref = refs[0]
    kv_refs = refs[1:1 + 2 * n_seg]
    o_ref = refs[1 + 2 * n_seg]
    s_scr = refs[2 + 2 * n_seg]
    n_sub = q_ref.shape[2] // tq_sub
    units = [(hh, j) for j in range(n_sub) for hh in range(2)]
    chunks = []
    off = 0
    for seg in range(n_seg):
        lk = kv_refs[2 * seg].shape[2]
        for st in range(0, lk, kc):
            sz = min(kc, lk - st)
            chunks.append((seg, st, sz, off))
            off += sz
    nch = len(chunks)
    mvec = {}
    col_max = {}
    acc = {}
    results = {}
    pv_hist = []
    depth = 5

    def score_chunk(u, c):
        hh, j = units[u]
        seg, st, sz, off = chunks[c]
        q = q_ref[0, hh, j * tq_sub:(j + 1) * tq_sub, :]
        k = kv_refs[2 * seg][0, hh, st:st + sz, :]
        s = lax.dot_general(k, q, (((1,), (1,)), ((), ())), preferred_element_type=F32)
        s_scr[u % 2, off:off + sz, :] = s
        cm = jnp.max(s.reshape(sz // SUBLANES, SUBLANES, tq_sub), axis=0)
        mvec[u] = cm if c == 0 else jnp.maximum(mvec[u], cm)
        if c == nch - 1:
            col_max[u] = jnp.max(mvec[u], axis=0, keepdims=True)

    def weight_chunk(u, c):
        hh, j = units[u]
        seg, st, sz, off = chunks[c]
        m = col_max[u]
        if len(pv_hist) >= depth:
            bits = pltpu.bitcast(pv_hist[-depth][MLA_V:MLA_V + 1], jnp.uint32)
            m = m + ((bits >> 16) >> 16).astype(F32)
        p = jnp.exp2((s_scr[u % 2, off:off + sz, :] - m).astype(BF16))
        vt = kv_refs[2 * seg + 1][0, 0, hh * V_ROWS:(hh + 1) * V_ROWS, st:st + sz]
        pv = jnp.dot(vt, p, preferred_element_type=F32)
        pv_hist.append(pv)
        acc[u] = pv if c == 0 else acc[u] + pv
        if c == nch - 1:
            results[units[u]] = acc[u][:MLA_V] / acc[u][MLA_V:MLA_V + 1]

    lead = nch + min(4, nch - 1)
    events = []
    for u in range(len(units)):
        for c in range(nch):
            events.append((u * nch + c - lead, 0, u, c))
            events.append((u * nch + c, 1, u, c))
    for _, kind, u, c in sorted(events):
        (weight_chunk if kind else score_chunk)(u, c)

    for j in range(n_sub):
        o_t = jnp.concatenate([results[(0, j)], results[(1, j)]], axis=0)
        o_ref[0, j * tq_sub:(j + 1) * tq_sub, :] = o_t.T.astype(o_ref.dtype)


def _attn_call(q, segs, tq):
    B, H, T, _ = q.shape
    tq_sub = min(tq, 256)
    in_specs = [pl.BlockSpec((1, 2, tq, LANES), lambda b, p, i: (b, p, i, 0))]
    args = [q]
    lk_total = 0
    for k, v in segs:
        lk = k.shape[2]
        lk_total += lk
        in_specs.append(pl.BlockSpec((1, 2, lk, LANES), lambda b, p, i: (b, p, 0, 0)))
        in_specs.append(pl.BlockSpec((1, 1, 2 * V_ROWS, lk), lambda b, p, i: (b, p, 0, 0)))
        args += [k, v]
    return pl.pallas_call(
        functools.partial(_attn_kernel, n_seg=len(segs), tq_sub=tq_sub, kc=256),
        grid=(B, H // 2, T // tq),
        in_specs=in_specs,
        out_specs=pl.BlockSpec((1, tq, LANES), lambda b, p, i: (b, i, p)),
        out_shape=jax.ShapeDtypeStruct((B, T, H * MLA_V), BF16),
        scratch_shapes=[pltpu.VMEM((2, lk_total, tq_sub), F32)],
        compiler_params=_cparams(3),
        name="attention",
    )(*args)


def _post_kernel(x_ref, a_ref, att_ref, p_ref, g1_ref, sh_ref, sc_ref, g2_ref, gn_ref, wo_ref,
                 w1_ref, w2_ref, o_ref, *, ff_chunk):
    d_lru = a_ref.shape[2]
    d_att = att_ref.shape[2]
    y = jnp.dot(a_ref[0], wo_ref[0:d_lru, :], preferred_element_type=F32)
    y += jnp.dot(att_ref[0], wo_ref[d_lru:d_lru + d_att, :], preferred_element_type=F32)
    y += jnp.dot(p_ref[0], wo_ref[d_lru + d_att:, :], preferred_element_type=F32)
    x1 = x_ref[0] + g1_ref[0] * y
    xn = x1 * lax.rsqrt(jnp.mean(x1 * x1, axis=-1, keepdims=True) + EPS)
    h = ((xn * gn_ref[...]) * (1.0 + sc_ref[0]) + sh_ref[0]).astype(BF16)
    d_ff = w1_ref.shape[1]
    acc = None
    for c in range(d_ff // ff_chunk):
        u = jnp.dot(h, w1_ref[:, c * ff_chunk:(c + 1) * ff_chunk], preferred_element_type=F32)
        u = jnp.square(jnp.maximum(u, 0.0)).astype(BF16)
        part = jnp.dot(u, w2_ref[c * ff_chunk:(c + 1) * ff_chunk, :], preferred_element_type=F32)
        acc = part if acc is None else acc + part
    o_ref[0] = x1 + g2_ref[0] * acc


def _post_call(x, a, att, p, g1, sh, sc, g2, gn, lw, tm):
    B, T, D = x.shape
    tok = lambda w: pl.BlockSpec((1, tm, w), lambda b, i: (b, i, 0))
    vec = pl.BlockSpec((1, 1, D), lambda b, i: (b, 0, 0))
    single = lambda shape: pl.BlockSpec(shape, lambda b, i: (0,) * len(shape),
                                        pipeline_mode=pl.Buffered(1))
    return pl.pallas_call(
        functools.partial(_post_kernel, ff_chunk=1024),
        grid=(B, T // tm),
        in_specs=[tok(D), tok(a.shape[2]), tok(att.shape[2]), tok(p.shape[2]), vec, vec, vec, vec,
                  _const_spec((1, D)), single(lw["w_out"].shape), single(lw["w_ff1"].shape),
                  single(lw["w_ff2"].shape)],
        out_specs=tok(D),
        out_shape=jax.ShapeDtypeStruct((B, T, D), F32),
        compiler_params=_cparams(2),
        name="out_proj_mlp",
    )(x, a, att, p, g1, sh, sc, g2, gn, lw["w_out"], lw["w_ff1"], lw["w_ff2"])


def _rope_partner():
    j = np.arange(MLA_ROPE)
    half = (j % (2 * ROPE_FREQS)) // ROPE_FREQS
    return np.where(half == 0, j + ROPE_FREQS, j - ROPE_FREQS)


def _block_diag(w):
    n, a, b = w.shape
    eye = jnp.asarray(np.eye(n, dtype=np.float32), w.dtype)
    return (w[:, :, None, :] * eye[:, None, :, None]).reshape(n * a, n * b)


def _layer_weights(l, w_in, conv_w, conv_b, lru_w_a, lru_b_a, lru_w_x, lru_b_x, lru_lambda, g_q_lat,
                   w_uq, g_kv_lat, w_ukv, g_qn, g_kn, w_pool, pool_scale, w_out, w_ff1, w_ff2):
    d = w_in.shape[1]
    d_lru = conv_w.shape[2]
    q_rank = w_uq.shape[1]
    kv_rank = w_ukv.shape[1]
    heads = w_uq.shape[2] // MLA_QK
    partner = _rope_partner()
    o_kr = 2 * d_lru + q_rank + kv_rank
    wi = w_in[l]
    kr = wi[:, o_kr:o_kr + MLA_ROPE]
    krb = jnp.concatenate([jnp.zeros((d, MLA_NOPE), F32), kr, kr[:, partner]], axis=1)
    w_in_p = jnp.concatenate([wi[:, :o_kr], krb, wi[:, o_kr + MLA_ROPE:]], axis=1).astype(BF16)
    wq = w_uq[l].reshape(q_rank, heads, MLA_QK)
    pad = jnp.zeros((q_rank, heads, HEAD_PAD - MLA_QK), F32)
    wq_main = jnp.concatenate([wq, pad], axis=2)
    wq_partner = wq[:, :, MLA_NOPE + partner]
    wq = jnp.concatenate([wq_main.reshape(q_rank, heads * HEAD_PAD),
                          wq_partner.reshape(q_rank, heads * MLA_ROPE)], axis=1)
    wkv = w_ukv[l].reshape(kv_rank, heads, MLA_NOPE + MLA_V)
    wk = jnp.concatenate([wkv[:, :, :MLA_NOPE], jnp.zeros((kv_rank, heads, HEAD_PAD - MLA_NOPE), F32)],
                         axis=2).reshape(kv_rank, heads * HEAD_PAD)
    wv = wkv[:, :, MLA_NOPE:].reshape(kv_rank, heads * MLA_V)
    zpad = jnp.zeros((HEAD_PAD - MLA_QK,), F32)
    znope = jnp.zeros((MLA_NOPE,), F32)
    q_scale = LOG2E / math.sqrt(MLA_QK)
    gq = jnp.concatenate([g_qn[l], zpad]) * q_scale
    gq_sw = jnp.concatenate([znope, g_qn[l][MLA_NOPE + partner], zpad]) * q_scale
    gk = jnp.concatenate([g_kn[l], zpad])
    gk_sw = jnp.concatenate([znope, g_kn[l][MLA_NOPE + partner], zpad])
    head_ones = jnp.asarray(np.kron(np.eye(2), np.ones((HEAD_PAD, HEAD_PAD))), BF16)
    w_gate = jnp.concatenate([_block_diag(lru_w_a[l, 0]), _block_diag(lru_w_x[l, 0]),
                              _block_diag(lru_w_a[l, 1]), _block_diag(lru_w_x[l, 1])], axis=1)
    b_gate = jnp.concatenate([lru_b_a[l, 0], lru_b_x[l, 0], lru_b_a[l, 1], lru_b_x[l, 1]])
    return dict(
        heads=heads, d_lru=d_lru, q_rank=q_rank, kv_rank=kv_rank,
        w_in=w_in_p, g_q_lat=g_q_lat[l][None], w_uq=wq.astype(BF16), g_kv_lat=g_kv_lat[l][None],
        w_k=wk.astype(BF16), w_v=wv.astype(BF16), head_gains=jnp.stack([gq, gq_sw, gk, gk_sw]),
        head_ones=head_ones,
        conv_w=conv_w[l], conv_b=conv_b[l][None], w_gate=w_gate.astype(BF16), b_gate=b_gate[None],
        lam=lru_lambda[l], w_pool=_block_diag(w_pool[l]).astype(BF16), pool_scale=pool_scale[l][None],
        w_out=w_out[l].astype(BF16), w_ff1=w_ff1[l].astype(BF16), w_ff2=w_ff2[l].astype(BF16))


def _rope_tables(T):
    f32 = np.float32
    j = np.arange(MLA_ROPE)
    axis = j // (2 * ROPE_FREQS)
    half = (j % (2 * ROPE_FREQS)) // ROPE_FREQS
    f = j % ROPE_FREQS
    t = np.arange(T)
    pos = np.stack([(t // GRID_W).astype(f32), (t % GRID_W).astype(f32)], axis=-1)
    freqs = np.power(f32(ROPE_BASE), -np.arange(ROPE_FREQS, dtype=f32) / f32(ROPE_FREQS)).astype(f32)
    ang = (pos[:, axis] * freqs[f]).astype(f32)
    sign = np.where(half == 0, -1.0, 1.0).astype(f32)
    ct = np.concatenate([np.ones((T, MLA_NOPE), f32), np.cos(ang).astype(f32),
                         np.zeros((T, HEAD_PAD - MLA_QK), f32)], axis=1)
    st = np.concatenate([np.zeros((T, MLA_NOPE), f32), (np.sin(ang) * sign).astype(f32),
                         np.zeros((T, HEAD_PAD - MLA_QK), f32)], axis=1)
    return jnp.asarray(ct), jnp.asarray(st)


def _no_rope_tables(T):
    lane = np.arange(HEAD_PAD)
    ct = np.broadcast_to((lane < MLA_QK).astype(np.float32), (T, HEAD_PAD))
    return jnp.asarray(ct), jnp.zeros((T, HEAD_PAD), F32)


def _tile(n, pref):
    t = min(n, pref)
    while n % t:
        t //= 2
    return t


def kernel(x, c, ctx, c_ctx, w_mod, b_mod, g_norm1, g_norm2, w_in, conv_w, conv_b, lru_w_a, lru_b_a,
           lru_w_x, lru_b_x, lru_lambda, g_q_lat, w_uq, g_kv_lat, w_ukv, g_qn, g_kn, w_pool, pool_scale,
           w_out, w_ff1, w_ff2):
    B, L, D = x.shape
    Lc = ctx.shape[1]
    depth = w_mod.shape[0]
    d_lru = conv_w.shape[2]

    rows = -(-(B + 1) // SUBLANES) * SUBLANES
    cvec = jnp.concatenate([c, c_ctx[None], jnp.zeros((rows - B - 1, D), F32)], axis=0)
    mods = _mod_call(cvec, w_mod, b_mod)

    rope_l = _rope_tables(L)
    rope_c = _no_rope_tables(Lc)
    tm_l, tm_c = _tile(L, 1024), _tile(Lc, 1024)
    tp_l, tp_c = _tile(L, 1024), _tile(Lc, 1024)
    tq_l, tq_c = _tile(L, 2048), _tile(Lc, 2048)
    tc_l, tc_c = _tile(L, 256), _tile(Lc, 256)

    h = ctx
    zeros_state = jnp.zeros((B, 2, d_lru), F32)
    for l in range(depth):
        last = l == depth - 1
        lw = _layer_weights(l, w_in, conv_w, conv_b, lru_w_a, lru_b_a, lru_w_x, lru_b_x, lru_lambda,
                            g_q_lat, w_uq, g_kv_lat, w_ukv, g_qn, g_kn, w_pool, pool_scale, w_out,
                            w_ff1, w_ff2)
        ml = mods[l, :B].reshape(B, 1, 6, D)
        mc = jnp.broadcast_to(mods[l, B].reshape(1, 1, 6, D), (B, 1, 6, D))
        sh1, sc1, g1, sh2, sc2, g2 = (ml[:, :, i] for i in range(6))
        csh1, csc1, cg1, csh2, csc2, cg2 = (mc[:, :, i] for i in range(6))
        gn1 = g_norm1[l][None]
        gn2 = g_norm2[l][None]

        lx_c, lg_c, pu_c, q_c, k_c, v_c = _inproj_call(h, csh1, csc1, gn1, lw, rope_c, tm_c)
        a_c, p_c, hfin = _seq_call(lx_c, lg_c, pu_c, zeros_state, lw, tc_c)
        lx_l, lg_l, pu_l, q_l, k_l, v_l = _inproj_call(x, sh1, sc1, gn1, lw, rope_l, tm_l)
        a_l, p_l, _ = _seq_call(lx_l, lg_l, pu_l, hfin, lw, tc_l)
        att_l = _attn_call(q_l, [(k_c, v_c), (k_l, v_l)], tq_l)
        x = _post_call(x, a_l, att_l, p_l, g1, sh2, sc2, g2, gn2, lw, tp_l)
        if not last:
            att_c = _attn_call(q_c, [(k_c, v_c)], tq_c)
            h = _post_call(h, a_c, att_c, p_c, cg1, csh2, csc2, cg2, gn2, lw, tp_c)
    return x
```

```python
import functools
import math

import jax
import jax.numpy as jnp
import numpy as np
from jax import lax
from jax.experimental import pallas as pl
from jax.experimental.pallas import tpu as pltpu

F32 = jnp.float32
BF16 = jnp.bfloat16

EPS = 1e-6
GRID_W = 64
LRU_HEADS = 4
LRU_C = 8.0
CONV_W = 4
MLA_V = 64
MLA_NOPE = 64
MLA_ROPE = 32
MLA_QK = MLA_NOPE + MLA_ROPE
ROPE_FREQS = MLA_ROPE // 4
ROPE_BASE = 10000.0
POOL_WINDOWS = (2, 4, 8, 16)

LANES = 128
SUBLANES = 8
MXU_DIM = 256
ROW_BLOCK = MXU_DIM
ATTN_Q_UNIT = MXU_DIM
ATTN_KEY_CHUNK = MXU_DIM
ATTN_Q_STEP = 2048
TOKEN_STEP = 1024
FF_CHUNK = 1024
MOD_COLS = 1024
SCORE_LEAD = 4
ORDER_DEPTH = 5
HEAD_PAD = LANES
BF16_ROWS = 16
V_ROWS = MLA_V + BF16_ROWS
HALO = 16
SCAN_GROUP = 4
SCAN_BLOCK = SCAN_GROUP * SUBLANES
VMEM_LIMIT = 56 * 1024 * 1024

LOG2E = 1.4426950408889634


def _cparams(n_grid, flags=None):
    return pltpu.CompilerParams(dimension_semantics=("arbitrary",) * n_grid,
                                vmem_limit_bytes=VMEM_LIMIT, flags=flags)


def _const_spec(shape):
    zeros = (0,) * len(shape)
    return pl.BlockSpec(shape, lambda *_: zeros)


def _mod_kernel(c_ref, w_ref, b_ref, o_ref):
    c = c_ref[...]
    act = (c * jax.nn.sigmoid(c)).astype(BF16)
    o_ref[0] = jnp.dot(act, w_ref[0].astype(BF16), preferred_element_type=F32) + b_ref[0]


def _mod_call(cvec, w_mod, b_mod):
    depth, d, n = w_mod.shape
    rows = cvec.shape[0]
    tn = MOD_COLS
    return pl.pallas_call(
        _mod_kernel,
        grid=(depth, n // tn),
        in_specs=[pl.BlockSpec((rows, d), lambda l, j: (0, 0)),
                  pl.BlockSpec((1, d, tn), lambda l, j: (l, 0, j)),
                  pl.BlockSpec((1, 1, tn), lambda l, j: (l, 0, j))],
        out_specs=pl.BlockSpec((1, rows, tn), lambda l, j: (l, 0, j)),
        out_shape=jax.ShapeDtypeStruct((depth, rows, n), F32),
        compiler_params=_cparams(2),
        name="adaln_mod",
    )(cvec, w_mod, b_mod.reshape(depth, 1, n))


def _inproj_kernel(x_ref, sh_ref, sc_ref, g1_ref, win_ref, gql_ref, wuq_ref, gkvl_ref, wk_ref,
                   wv_ref, ones_ref, gains_ref, ct_ref, st_ref,
                   lx_ref, lg_ref, pu_ref, q_ref, k_ref, v_ref, *, d_lru, q_rank, kv_rank, heads, rsub):
    lane = lax.broadcasted_iota(jnp.int32, (1, HEAD_PAD), 1)
    rot = (lane >= MLA_NOPE) & (lane < MLA_QK)
    gains = gains_ref[...]
    ones = ones_ref[...]
    inv_qk = 1.0 / MLA_QK
    n_main = heads * HEAD_PAD
    pair = 2 * HEAD_PAD
    extra = (lax.broadcasted_iota(jnp.int32, (BF16_ROWS, rsub), 0) == 0).astype(F32)

    for r0 in range(0, x_ref.shape[1], rsub):
        rows = slice(r0, r0 + rsub)
        x = x_ref[0, rows, :]
        xn = x * lax.rsqrt(jnp.mean(x * x, axis=-1, keepdims=True) + EPS)
        hx = (xn * g1_ref[...]) * (1.0 + sc_ref[0]) + sh_ref[0]
        z = jnp.dot(hx.astype(BF16), win_ref[...], preferred_element_type=F32)
        o = 0
        lx_ref[0, rows, :] = z[:, o:o + d_lru]; o += d_lru
        lg_ref[0, rows, :] = z[:, o:o + d_lru]; o += d_lru
        ql = z[:, o:o + q_rank]; o += q_rank
        kvl = z[:, o:o + kv_rank]; o += kv_rank
        krb = z[:, o:o + HEAD_PAD]; o += HEAD_PAD
        pu_ref[0, rows, :] = z[:, o:]

        qln = ql * lax.rsqrt(jnp.mean(ql * ql, axis=-1, keepdims=True) + EPS) * gql_ref[...]
        qf = jnp.dot(qln.astype(BF16), wuq_ref[...], preferred_element_type=F32)
        kvn = kvl * lax.rsqrt(jnp.mean(kvl * kvl, axis=-1, keepdims=True) + EPS) * gkvl_ref[...]
        kvn = kvn.astype(BF16)
        kf = jnp.dot(kvn, wk_ref[...], preferred_element_type=F32)
        vf = jnp.dot(kvn, wv_ref[...], preferred_element_type=F32)

        kr = jnp.where(rot, krb, 0.0)
        kr_sw = pltpu.roll(krb, HEAD_PAD - MLA_ROPE, axis=1)
        kr2 = jnp.concatenate([kr, kr], axis=1)
        ct, st = ct_ref[rows, :], st_ref[rows, :]
        aq, bq, ak, bk = ct * gains[0:1], st * gains[1:2], ct * gains[2:3], st * gains[3:4]
        for hp in range(heads // 2):
            qa = qf[:, hp * pair:(hp + 1) * pair]
            ss = jnp.dot((qa * qa).astype(BF16), ones, preferred_element_type=F32)
            rq = lax.rsqrt(ss * inv_qk + EPS)
            ka = kf[:, hp * pair:(hp + 1) * pair] + kr2
            ss = jnp.dot((ka * ka).astype(BF16), ones, preferred_element_type=F32)
            rk = lax.rsqrt(ss * inv_qk + EPS)
            for e in range(2):
                h = 2 * hp + e
                lo, hi = e * HEAD_PAD, (e + 1) * HEAD_PAD
                grp = qf[:, n_main + (h // 4) * HEAD_PAD:n_main + (h // 4 + 1) * HEAD_PAD]
                shift = (MLA_NOPE - (h % 4) * MLA_ROPE) % HEAD_PAD
                qb = pltpu.roll(grp, shift, axis=1) if shift else grp
                q_ref[0, h, rows, :] = (rq[:, lo:hi] * (qa[:, lo:hi] * aq + qb * bq)).astype(BF16)
                k_ref[0, h, rows, :] = (rk[:, lo:hi] * (ka[:, lo:hi] * ak + kr_sw * bk)).astype(BF16)
        for p in range(heads // 2):
            vt = vf[:, p * LANES:(p + 1) * LANES].T
            v_ref[0, p, :, rows] = jnp.concatenate(
                [vt[:MLA_V], extra, vt[MLA_V:], extra], axis=0).astype(BF16)


def _inproj_call(x, sh, sc, g1, lw, tables, tm):
    B, T, D = x.shape
    pos = pl.BlockSpec((tm, HEAD_PAD), lambda b, i: (i, 0))
    heads = lw["heads"]
    d_lru, q_rank, kv_rank = lw["d_lru"], lw["q_rank"], lw["kv_rank"]
    d_pool = lw["w_in"].shape[1] - (2 * d_lru + q_rank + kv_rank + HEAD_PAD)
    tok = lambda w: pl.BlockSpec((1, tm, w), lambda b, i: (b, i, 0))
    vec = pl.BlockSpec((1, 1, D), lambda b, i: (b, 0, 0))
    hd = lambda n: pl.BlockSpec((1, n, tm, LANES), lambda b, i: (b, 0, i, 0))
    kern = functools.partial(_inproj_kernel, d_lru=d_lru, q_rank=q_rank, kv_rank=kv_rank, heads=heads,
                             rsub=min(tm, ROW_BLOCK))
    return pl.pallas_call(
        kern,
        grid=(B, T // tm),
        in_specs=[tok(D), vec, vec, _const_spec((1, D)), _const_spec(lw["w_in"].shape),
                  _const_spec((1, q_rank)), _const_spec(lw["w_uq"].shape),
                  _const_spec((1, kv_rank)), _const_spec(lw["w_k"].shape), _const_spec(lw["w_v"].shape),
                  _const_spec(lw["head_ones"].shape), _const_spec(lw["head_gains"].shape), pos, pos],
        out_specs=[tok(d_lru), tok(d_lru), tok(d_pool), hd(heads), hd(heads),
                   pl.BlockSpec((1, heads // 2, 2 * V_ROWS, tm), lambda b, i: (b, 0, 0, i))],
        out_shape=[jax.ShapeDtypeStruct((B, T, d_lru), F32),
                   jax.ShapeDtypeStruct((B, T, d_lru), F32),
                   jax.ShapeDtypeStruct((B, T, d_pool), F32),
                   jax.ShapeDtypeStruct((B, heads, T, LANES), BF16),
                   jax.ShapeDtypeStruct((B, heads, T, LANES), BF16),
                   jax.ShapeDtypeStruct((B, heads // 2, 2 * V_ROWS, T), BF16)],
        compiler_params=_cparams(2),
        name="in_proj",
    )(x, sh, sc, g1, lw["w_in"], lw["g_q_lat"], lw["w_uq"], lw["g_kv_lat"], lw["w_k"], lw["w_v"],
      lw["head_ones"], lw["head_gains"], *tables)


def _roll_rows(a, shift):
    return pltpu.roll(a, shift % a.shape[0], axis=0)


def _scan8(a, b, reverse):
    n = a.shape[0]
    a3 = a.reshape(n // SUBLANES, SUBLANES, a.shape[1])
    b3 = b.reshape(n // SUBLANES, SUBLANES, b.shape[1])
    rid = lax.broadcasted_iota(jnp.int32, a3.shape, 1)
    for s in (1, 2, 4):
        if reverse:
            m = rid < SUBLANES - s
            sh = SUBLANES - s
        else:
            m = rid >= s
            sh = s
        a_sh = pltpu.roll(a3, sh, axis=1)
        b_sh = pltpu.roll(b3, sh, axis=1)
        b3 = jnp.where(m, a3 * b_sh + b3, b3)
        a3 = jnp.where(m, a3 * a_sh, a3)
    return a3, b3


def _chunk_scan(a, b, h, sa_ref, sb_ref, out_ref, row0, reverse):
    tc, C = a.shape
    rid = lax.broadcasted_iota(jnp.int32, (SUBLANES, LANES), 0)
    order = list(range(SCAN_GROUP - 1, -1, -1) if reverse else range(SCAN_GROUP))
    n_blocks = tc // SCAN_BLOCK
    blocks = range(n_blocks - 1, -1, -1) if reverse else range(n_blocks)
    for s in range(C // LANES):
        sa_ref[s] = a[:, s * LANES:(s + 1) * LANES]
        sb_ref[s] = b[:, s * LANES:(s + 1) * LANES]
    h_out = []
    for s in range(C // LANES):
        hs = h[:, s * LANES:(s + 1) * LANES]
        for k in blocks:
            base = k * SCAN_BLOCK
            av = [sa_ref[s, pl.ds(base + j, SUBLANES, stride=SCAN_GROUP), :] for j in range(SCAN_GROUP)]
            bv = [sb_ref[s, pl.ds(base + j, SUBLANES, stride=SCAN_GROUP), :] for j in range(SCAN_GROUP)]
            hh, pp = {}, {}
            prev = None
            for j in order:
                if prev is None:
                    hh[j], pp[j] = bv[j], av[j]
                else:
                    hh[j], pp[j] = av[j] * hh[prev] + bv[j], av[j] * pp[prev]
                prev = j
            pc, hc = _scan8(pp[prev], hh[prev], reverse)
            end = hc.reshape(SUBLANES, LANES) + pc.reshape(SUBLANES, LANES) * hs
            if reverse:
                enter = jnp.where(rid == SUBLANES - 1, hs, pltpu.roll(end, SUBLANES - 1, axis=0))
                edge = end[0:1]
            else:
                enter = jnp.where(rid == 0, hs, pltpu.roll(end, 1, axis=0))
                edge = end[SUBLANES - 1:SUBLANES]
            for j in range(SCAN_GROUP):
                out_ref[s, pl.ds(row0 + base + j, SUBLANES, stride=SCAN_GROUP), :] = hh[j] + pp[j] * enter
            hs = jnp.broadcast_to(edge, (SUBLANES, LANES))
        h_out.append(hs)
    return jnp.concatenate(h_out, axis=1)


def _seq_kernel(lx_ref, lg_ref, pu_ref, h0_ref, cw_ref, cb_ref, wg_ref, bg_ref, lam_ref, wp_ref,
                ps_ref, a_out_ref, p_out_ref, hfin_ref, xpad_ref, upad_ref, hf_ref, xc_ref,
                sa_ref, sb_ref, hb_ref, *, tc):
    T = lx_ref.shape[1]
    C = lx_ref.shape[2]
    nc = T // tc
    zeros_halo = jnp.zeros((HALO, C), F32)
    xpad_ref[0:HALO, :] = zeros_halo
    xpad_ref[HALO + T:HALO + T + HALO, :] = zeros_halo
    upad_ref[0:HALO, :] = zeros_halo
    upad_ref[HALO + T:HALO + T + HALO, :] = zeros_halo

    def fill(c, carry):
        r0 = pl.multiple_of(c * tc, tc)
        xpad_ref[pl.ds(HALO + r0, tc), :] = lx_ref[0, pl.ds(r0, tc), :]
        upad_ref[pl.ds(HALO + r0, tc), :] = pu_ref[0, pl.ds(r0, tc), :]
        return carry

    lax.fori_loop(0, nc, fill, 0)

    lam = lam_ref[...]
    coef = -LRU_C * (jnp.maximum(-lam, 0.0) + jnp.log(1.0 + jnp.exp(-jnp.abs(lam))))
    cw = cw_ref[...]
    cb = cb_ref[...]
    bg = bg_ref[...]

    def conv_chunk(r0):
        w = xpad_ref[pl.ds(r0 + HALO - SUBLANES, tc + 2 * SUBLANES), :]
        y = (cw[0:1] * _roll_rows(w, 1) + cw[1:2] * w + cw[2:3] * _roll_rows(w, -1)
             + cw[3:4] * _roll_rows(w, -2))
        return y[SUBLANES:SUBLANES + tc] + cb

    def coeffs(xc, d):
        g = jnp.dot(xc.astype(BF16), wg_ref[:, 2 * C * d:2 * C * (d + 1)],
                    preferred_element_type=F32) + bg[:, 2 * C * d:2 * C * (d + 1)]
        r = jax.nn.sigmoid(g[:, :C])
        i = jax.nn.sigmoid(g[:, C:])
        log_a = coef[d:d + 1] * r
        a = jnp.exp(log_a)
        b = jnp.exp2(0.5 * jnp.log2(1.0 - a * a)) * (i * xc)
        return a, b

    lane = lax.broadcasted_iota(jnp.int32, (1, C), 1)
    gdim = C // len(POOL_WINDOWS)

    inv_win = None
    for g, win in enumerate(POOL_WINDOWS):
        inv_g = jnp.full((1, C), 1.0 / win, F32)
        inv_win = inv_g if inv_win is None else jnp.where(lane >= g * gdim, inv_g, inv_win)

    def pool_rows(r0, n, edge):
        w = upad_ref[pl.ds(r0, n + 2 * HALO), :]
        s2 = w + _roll_rows(w, 1)
        s4 = _roll_rows(s2, 1) + _roll_rows(s2, -1)
        s8 = _roll_rows(s4, 2) + _roll_rows(s4, -2)
        s16 = _roll_rows(s8, 4) + _roll_rows(s8, -4)
        sel = s2[HALO:HALO + n]
        for g, s in enumerate((s4, s8, s16), start=1):
            sel = jnp.where(lane >= g * gdim, s[HALO:HALO + n], sel)
        if edge:
            t = r0 + lax.broadcasted_iota(jnp.int32, (n, 1), 0)
            cnt = None
            for g, win in enumerate(POOL_WINDOWS):
                cg = (jnp.minimum(t - win // 2 + win, T) - jnp.maximum(t - win // 2, 0)).astype(F32)
                cg = jnp.broadcast_to(cg, (n, C))
                cnt = cg if cnt is None else jnp.where(lane >= g * gdim, cg, cnt)
            mean = sel / cnt
        else:
            mean = sel * inv_win
        mixed = mean - w[HALO:HALO + n]
        y = jnp.dot(mixed.astype(BF16), wp_ref[...], preferred_element_type=F32) * ps_ref[...]
        p_out_ref[0, pl.ds(r0, n), :] = y.astype(p_out_ref.dtype)

    per = 2 if nc % 2 == 0 else 1
    n_slab = C // LANES

    def fwd(i, h):
        for par in range(per):
            r0 = pl.multiple_of((per * i + par) * tc, tc)
            xc = conv_chunk(r0)
            xc_ref[pl.ds(r0, tc), :] = xc
            a, b = coeffs(xc, 0)
            h = _chunk_scan(a, b, h, sa_ref.at[par], sb_ref.at[par], hf_ref, r0, False)
        return h

    h = lax.fori_loop(0, nc // per, fwd, jnp.broadcast_to(h0_ref[0, 0:1, :], (SUBLANES, C)))
    hfin_ref[0, 0:1, :] = h[0:1]

    def bwd(i, h):
        for par in range(per):
            r0 = pl.multiple_of((nc - 1 - (per * i + par)) * tc, tc)
            rows = pl.ds(r0, tc)
            pool_rows(r0, tc, False)
            a, b = coeffs(xc_ref[rows, :], 1)
            h = _chunk_scan(a, b, h, sa_ref.at[par], sb_ref.at[par], hb_ref.at[par], 0, True)
            y = (jnp.concatenate([hb_ref[par, s] for s in range(n_slab)], axis=1)
                 + jnp.concatenate([hf_ref[s, rows, :] for s in range(n_slab)], axis=1))
            a_out_ref[0, rows, :] = (y * jax.nn.gelu(lg_ref[0, rows, :])).astype(a_out_ref.dtype)
        return h

    h = lax.fori_loop(0, nc // per, bwd, jnp.broadcast_to(h0_ref[0, 1:2, :], (SUBLANES, C)))
    hfin_ref[0, 1:2, :] = h[0:1]

    pool_rows(0, BF16_ROWS, True)
    pool_rows(T - BF16_ROWS, BF16_ROWS, True)


def _seq_call(lx, lg, pu, h0, lw, tc):
    B, T, C = lx.shape
    seq = pl.BlockSpec((1, T, C), lambda b: (b, 0, 0))
    st = pl.BlockSpec((1, 2, C), lambda b: (b, 0, 0))
    return pl.pallas_call(
        functools.partial(_seq_kernel, tc=tc),
        grid=(B,),
        in_specs=[seq, seq, seq, st, _const_spec((CONV_W, C)), _const_spec((1, C)),
                  _const_spec((C, 4 * C)), _const_spec((1, 4 * C)), _const_spec((2, C)),
                  _const_spec((C, C)), _const_spec((1, C))],
        out_specs=[seq, seq, st],
        out_shape=[jax.ShapeDtypeStruct((B, T, C), BF16), jax.ShapeDtypeStruct((B, T, C), BF16),
                   jax.ShapeDtypeStruct((B, 2, C), F32)],
        scratch_shapes=[pltpu.VMEM((T + 2 * HALO, C), F32), pltpu.VMEM((T + 2 * HALO, C), F32),
                        pltpu.VMEM((C // LANES, T, LANES), F32), pltpu.VMEM((T, C), F32),
                        pltpu.VMEM((2, C // LANES, tc, LANES), F32),
                        pltpu.VMEM((2, C // LANES, tc, LANES), F32),
                        pltpu.VMEM((2, C // LANES, tc, LANES), F32)],
        compiler_params=_cparams(1),
        name="seq_mix",
    )(lx, lg, pu, h0, lw["conv_w"], lw["conv_b"], lw["w_gate"], lw["b_gate"], lw["lam"],
      lw["w_pool"], lw["pool_scale"])


def _attn_kernel(*refs, n_seg, tq_sub, kc):
    q_ref = refs[0]
    kv_refs = refs[1:1 + 2 * n_seg]
    o_ref = refs[1 + 2 * n_seg]
    s_scr = refs[2 + 2 * n_seg]
    n_sub = q_ref.shape[2] // tq_sub
    units = [(hh, j) for j in range(n_sub) for hh in range(2)]
    chunks = []
    off = 0
    for seg in range(n_seg):
        lk = kv_refs[2 * seg].shape[2]
        for st in range(0, lk, kc):
            sz = min(kc, lk - st)
            chunks.append((seg, st, sz, off))
            off += sz
    nch = len(chunks)
    mvec = {}
    col_max = {}
    acc = {}
    results = {}
    pv_hist = []

    def score_chunk(u, c):
        hh, j = units[u]
        seg, st, sz, off = chunks[c]
        q = q_ref[0, hh, j * tq_sub:(j + 1) * tq_sub, :]
        k = kv_refs[2 * seg][0, hh, st:st + sz, :]
        s = lax.dot_general(k, q, (((1,), (1,)), ((), ())), preferred_element_type=F32)
        s_scr[u % 2, off:off + sz, :] = s
        cm = jnp.max(s.reshape(sz // SUBLANES, SUBLANES, tq_sub), axis=0)
        mvec[u] = cm if c == 0 else jnp.maximum(mvec[u], cm)
        if c == nch - 1:
            col_max[u] = jnp.max(mvec[u], axis=0, keepdims=True)

    def weight_chunk(u, c):
        hh, j = units[u]
        seg, st, sz, off = chunks[c]
        m = col_max[u]
        if len(pv_hist) >= ORDER_DEPTH:
            bits = pltpu.bitcast(pv_hist[-ORDER_DEPTH][MLA_V:MLA_V + 1], jnp.uint32)
            m = m + ((bits >> 16) >> 16).astype(F32)
        p = jnp.exp2((s_scr[u % 2, off:off + sz, :] - m).astype(BF16))
        vt = kv_refs[2 * seg + 1][0, 0, hh * V_ROWS:(hh + 1) * V_ROWS, st:st + sz]
        pv = jnp.dot(vt, p, preferred_element_type=F32)
        pv_hist.append(pv)
        acc[u] = pv if c == 0 else acc[u] + pv
        if c == nch - 1:
            results[units[u]] = acc[u][:MLA_V] / acc[u][MLA_V:MLA_V + 1]

    lead = nch + min(SCORE_LEAD, nch - 1)
    events = []
    for u in range(len(units)):
        for c in range(nch):
            events.append((u * nch + c - lead, 0, u, c))
            events.append((u * nch + c, 1, u, c))
    for _, kind, u, c in sorted(events):
        (weight_chunk if kind else score_chunk)(u, c)

    for j in range(n_sub):
        o_t = jnp.concatenate([results[(0, j)], results[(1, j)]], axis=0)
        o_ref[0, j * tq_sub:(j + 1) * tq_sub, :] = o_t.T.astype(o_ref.dtype)


def _attn_call(q, segs, tq):
    B, H, T, _ = q.shape
    tq_sub = min(tq, ATTN_Q_UNIT)
    in_specs = [pl.BlockSpec((1, 2, tq, LANES), lambda b, p, i: (b, p, i, 0))]
    args = [q]
    lk_total = 0
    for k, v in segs:
        lk = k.shape[2]
        lk_total += lk
        in_specs.append(pl.BlockSpec((1, 2, lk, LANES), lambda b, p, i: (b, p, 0, 0)))
        in_specs.append(pl.BlockSpec((1, 1, 2 * V_ROWS, lk), lambda b, p, i: (b, p, 0, 0)))
        args += [k, v]
    return pl.pallas_call(
        functools.partial(_attn_kernel, n_seg=len(segs), tq_sub=tq_sub, kc=ATTN_KEY_CHUNK),
        grid=(B, H // 2, T // tq),
        in_specs=in_specs,
        out_specs=pl.BlockSpec((1, tq, LANES), lambda b, p, i: (b, i, p)),
        out_shape=jax.ShapeDtypeStruct((B, T, H * MLA_V), BF16),
        scratch_shapes=[pltpu.VMEM((2, lk_total, tq_sub), F32)],
        compiler_params=_cparams(3),
        name="attention",
    )(*args)


def _post_kernel(x_ref, a_ref, att_ref, p_ref, g1_ref, sh_ref, sc_ref, g2_ref, gn_ref, wo_ref,
                 w1_ref, w2_ref, o_ref, *, ff_chunk):
    d_lru = a_ref.shape[2]
    d_att = att_ref.shape[2]
    y = jnp.dot(a_ref[0], wo_ref[0:d_lru, :], preferred_element_type=F32)
    y += jnp.dot(att_ref[0], wo_ref[d_lru:d_lru + d_att, :], preferred_element_type=F32)
    y += jnp.dot(p_ref[0], wo_ref[d_lru + d_att:, :], preferred_element_type=F32)
    x1 = x_ref[0] + g1_ref[0] * y
    xn = x1 * lax.rsqrt(jnp.mean(x1 * x1, axis=-1, keepdims=True) + EPS)
    h = ((xn * gn_ref[...]) * (1.0 + sc_ref[0]) + sh_ref[0]).astype(BF16)
    d_ff = w1_ref.shape[1]
    acc = None
    for c in range(d_ff // ff_chunk):
        u = jnp.dot(h, w1_ref[:, c * ff_chunk:(c + 1) * ff_chunk], preferred_element_type=F32)
        u = jnp.square(jnp.maximum(u, 0.0)).astype(BF16)
        part = jnp.dot(u, w2_ref[c * ff_chunk:(c + 1) * ff_chunk, :], preferred_element_type=F32)
        acc = part if acc is None else acc + part
    o_ref[0] = x1 + g2_ref[0] * acc


def _post_call(x, a, att, p, g1, sh, sc, g2, gn, lw, tm):
    B, T, D = x.shape
    tok = lambda w: pl.BlockSpec((1, tm, w), lambda b, i: (b, i, 0))
    vec = pl.BlockSpec((1, 1, D), lambda b, i: (b, 0, 0))
    single = lambda shape: pl.BlockSpec(shape, lambda b, i: (0,) * len(shape),
                                        pipeline_mode=pl.Buffered(1))
    return pl.pallas_call(
        functools.partial(_post_kernel, ff_chunk=FF_CHUNK),
        grid=(B, T // tm),
        in_specs=[tok(D), tok(a.shape[2]), tok(att.shape[2]), tok(p.shape[2]), vec, vec, vec, vec,
                  _const_spec((1, D)), single(lw["w_out"].shape), single(lw["w_ff1"].shape),
                  single(lw["w_ff2"].shape)],
        out_specs=tok(D),
        out_shape=jax.ShapeDtypeStruct((B, T, D), F32),
        compiler_params=_cparams(2),
        name="out_proj_mlp",
    )(x, a, att, p, g1, sh, sc, g2, gn, lw["w_out"], lw["w_ff1"], lw["w_ff2"])


def _rope_partner():
    j = np.arange(MLA_ROPE)
    half = (j % (2 * ROPE_FREQS)) // ROPE_FREQS
    return np.where(half == 0, j + ROPE_FREQS, j - ROPE_FREQS)


def _block_diag(w):
    n, a, b = w.shape
    eye = jnp.asarray(np.eye(n, dtype=np.float32), w.dtype)
    return (w[:, :, None, :] * eye[:, None, :, None]).reshape(n * a, n * b)


def _layer_weights(l, w_in, conv_w, conv_b, lru_w_a, lru_b_a, lru_w_x, lru_b_x, lru_lambda, g_q_lat,
                   w_uq, g_kv_lat, w_ukv, g_qn, g_kn, w_pool, pool_scale, w_out, w_ff1, w_ff2):
    d = w_in.shape[1]
    d_lru = conv_w.shape[2]
    q_rank = w_uq.shape[1]
    kv_rank = w_ukv.shape[1]
    heads = w_uq.shape[2] // MLA_QK
    partner = _rope_partner()
    o_kr = 2 * d_lru + q_rank + kv_rank
    wi = w_in[l]
    kr = wi[:, o_kr:o_kr + MLA_ROPE]
    krb = jnp.concatenate([jnp.zeros((d, MLA_NOPE), F32), kr, kr[:, partner]], axis=1)
    w_in_p = jnp.concatenate([wi[:, :o_kr], krb, wi[:, o_kr + MLA_ROPE:]], axis=1).astype(BF16)
    wq = w_uq[l].reshape(q_rank, heads, MLA_QK)
    pad = jnp.zeros((q_rank, heads, HEAD_PAD - MLA_QK), F32)
    wq_main = jnp.concatenate([wq, pad], axis=2)
    wq_partner = wq[:, :, MLA_NOPE + partner]
    wq = jnp.concatenate([wq_main.reshape(q_rank, heads * HEAD_PAD),
                          wq_partner.reshape(q_rank, heads * MLA_ROPE)], axis=1)
    wkv = w_ukv[l].reshape(kv_rank, heads, MLA_NOPE + MLA_V)
    wk = jnp.concatenate([wkv[:, :, :MLA_NOPE], jnp.zeros((kv_rank, heads, HEAD_PAD - MLA_NOPE), F32)],
                         axis=2).reshape(kv_rank, heads * HEAD_PAD)
    wv = wkv[:, :, MLA_NOPE:].reshape(kv_rank, heads * MLA_V)
    zpad = jnp.zeros((HEAD_PAD - MLA_QK,), F32)
    znope = jnp.zeros((MLA_NOPE,), F32)
    q_scale = LOG2E / math.sqrt(MLA_QK)
    gq = jnp.concatenate([g_qn[l], zpad]) * q_scale
    gq_sw = jnp.concatenate([znope, g_qn[l][MLA_NOPE + partner], zpad]) * q_scale
    gk = jnp.concatenate([g_kn[l], zpad])
    gk_sw = jnp.concatenate([znope, g_kn[l][MLA_NOPE + partner], zpad])
    head_ones = jnp.asarray(np.kron(np.eye(2), np.ones((HEAD_PAD, HEAD_PAD))), BF16)
    w_gate = jnp.concatenate([_block_diag(lru_w_a[l, 0]), _block_diag(lru_w_x[l, 0]),
                              _block_diag(lru_w_a[l, 1]), _block_diag(lru_w_x[l, 1])], axis=1)
    b_gate = jnp.concatenate([lru_b_a[l, 0], lru_b_x[l, 0], lru_b_a[l, 1], lru_b_x[l, 1]])
    return dict(
        heads=heads, d_lru=d_lru, q_rank=q_rank, kv_rank=kv_rank,
        w_in=w_in_p, g_q_lat=g_q_lat[l][None], w_uq=wq.astype(BF16), g_kv_lat=g_kv_lat[l][None],
        w_k=wk.astype(BF16), w_v=wv.astype(BF16), head_gains=jnp.stack([gq, gq_sw, gk, gk_sw]),
        head_ones=head_ones,
        conv_w=conv_w[l], conv_b=conv_b[l][None], w_gate=w_gate.astype(BF16), b_gate=b_gate[None],
        lam=lru_lambda[l], w_pool=_block_diag(w_pool[l]).astype(BF16), pool_scale=pool_scale[l][None],
        w_out=w_out[l].astype(BF16), w_ff1=w_ff1[l].astype(BF16), w_ff2=w_ff2[l].astype(BF16))


def _rope_tables(T):
    f32 = np.float32
    j = np.arange(MLA_ROPE)
    axis = j // (2 * ROPE_FREQS)
    half = (j % (2 * ROPE_FREQS)) // ROPE_FREQS
    f = j % ROPE_FREQS
    t = np.arange(T)
    pos = np.stack([(t // GRID_W).astype(f32), (t % GRID_W).astype(f32)], axis=-1)
    freqs = np.power(f32(ROPE_BASE), -np.arange(ROPE_FREQS, dtype=f32) / f32(ROPE_FREQS)).astype(f32)
    ang = (pos[:, axis] * freqs[f]).astype(f32)
    sign = np.where(half == 0, -1.0, 1.0).astype(f32)
    ct = np.concatenate([np.ones((T, MLA_NOPE), f32), np.cos(ang).astype(f32),
                         np.zeros((T, HEAD_PAD - MLA_QK), f32)], axis=1)
    st = np.concatenate([np.zeros((T, MLA_NOPE), f32), (np.sin(ang) * sign).astype(f32),
                         np.zeros((T, HEAD_PAD - MLA_QK), f32)], axis=1)
    return jnp.asarray(ct), jnp.asarray(st)


def _no_rope_tables(T):
    lane = np.arange(HEAD_PAD)
    ct = np.broadcast_to((lane < MLA_QK).astype(np.float32), (T, HEAD_PAD))
    return jnp.asarray(ct), jnp.zeros((T, HEAD_PAD), F32)


def _tile(n, pref):
    t = min(n, pref)
    while n % t:
        t //= 2
    return t


def kernel(x, c, ctx, c_ctx, w_mod, b_mod, g_norm1, g_norm2, w_in, conv_w, conv_b, lru_w_a, lru_b_a,
           lru_w_x, lru_b_x, lru_lambda, g_q_lat, w_uq, g_kv_lat, w_ukv, g_qn, g_kn, w_pool, pool_scale,
           w_out, w_ff1, w_ff2):
    B, L, D = x.shape
    Lc = ctx.shape[1]
    depth = w_mod.shape[0]
    d_lru = conv_w.shape[2]

    rows = -(-(B + 1) // SUBLANES) * SUBLANES
    cvec = jnp.concatenate([c, c_ctx[None], jnp.zeros((rows - B - 1, D), F32)], axis=0)
    mods = _mod_call(cvec, w_mod, b_mod)

    rope_l = _rope_tables(L)
    rope_c = _no_rope_tables(Lc)
    tm_l, tm_c = _tile(L, TOKEN_STEP), _tile(Lc, TOKEN_STEP)
    tp_l, tp_c = _tile(L, TOKEN_STEP), _tile(Lc, TOKEN_STEP)
    tq_l, tq_c = _tile(L, ATTN_Q_STEP), _tile(Lc, ATTN_Q_STEP)
    tc_l, tc_c = _tile(L, ROW_BLOCK), _tile(Lc, ROW_BLOCK)

    h = ctx
    zeros_state = jnp.zeros((B, 2, d_lru), F32)
    for l in range(depth):
        last = l == depth - 1
        lw = _layer_weights(l, w_in, conv_w, conv_b, lru_w_a, lru_b_a, lru_w_x, lru_b_x, lru_lambda,
                            g_q_lat, w_uq, g_kv_lat, w_ukv, g_qn, g_kn, w_pool, pool_scale, w_out,
                            w_ff1, w_ff2)
        ml = mods[l, :B].reshape(B, 1, 6, D)
        mc = jnp.broadcast_to(mods[l, B].reshape(1, 1, 6, D), (B, 1, 6, D))
        sh1, sc1, g1, sh2, sc2, g2 = (ml[:, :, i] for i in range(6))
        csh1, csc1, cg1, csh2, csc2, cg2 = (mc[:, :, i] for i in range(6))
        gn1 = g_norm1[l][None]
        gn2 = g_norm2[l][None]

        lx_c, lg_c, pu_c, q_c, k_c, v_c = _inproj_call(h, csh1, csc1, gn1, lw, rope_c, tm_c)
        a_c, p_c, hfin = _seq_call(lx_c, lg_c, pu_c, zeros_state, lw, tc_c)
        lx_l, lg_l, pu_l, q_l, k_l, v_l = _inproj_call(x, sh1, sc1, gn1, lw, rope_l, tm_l)
        a_l, p_l, _ = _seq_call(lx_l, lg_l, pu_l, hfin, lw, tc_l)
        att_l = _attn_call(q_l, [(k_c, v_c), (k_l, v_l)], tq_l)
        x = _post_call(x, a_l, att_l, p_l, g1, sh2, sc2, g2, gn2, lw, tp_l)
        if not last:
            att_c = _attn_call(q_c, [(k_c, v_c)], tq_c)
            h = _post_call(h, a_c, att_c, p_c, cg1, csh2, csc2, cg2, gn2, lw, tp_c)
    return x
```

```python
import functools
import math

import jax
import jax.numpy as jnp
import numpy as np
from jax import lax
from jax.experimental import pallas as pl
from jax.experimental.pallas import tpu as pltpu

F32 = jnp.float32
BF16 = jnp.bfloat16

EPS = 1e-6
GRID_W = 64
LRU_HEADS = 4
LRU_C = 8.0
CONV_W = 4
MLA_V = 64
MLA_NOPE = 64
MLA_ROPE = 32
MLA_QK = MLA_NOPE + MLA_ROPE
ROPE_FREQS = MLA_ROPE // 4
ROPE_BASE = 10000.0
POOL_WINDOWS = (2, 4, 8, 16)

LANES = 128
SUBLANES = 8
MXU_DIM = 256
ROW_BLOCK = MXU_DIM
ATTN_Q_UNIT = MXU_DIM
ATTN_KEY_CHUNK = MXU_DIM
ATTN_Q_STEP = 4096
TOKEN_STEP = 1024
FF_CHUNK = 1024
MOD_COLS = 1024
SCORE_LEAD = 4
ORDER_DEPTH = 5
HEAD_PAD = LANES
BF16_ROWS = 16
V_ROWS = MLA_V + BF16_ROWS
HALO = 16
SCAN_GROUP = 4
SCAN_BLOCK = SCAN_GROUP * SUBLANES
VMEM_LIMIT = 56 * 1024 * 1024

LOG2E = 1.4426950408889634


def _cparams(n_grid, flags=None):
    return pltpu.CompilerParams(dimension_semantics=("arbitrary",) * n_grid,
                                vmem_limit_bytes=VMEM_LIMIT, flags=flags)


def _const_spec(shape):
    zeros = (0,) * len(shape)
    return pl.BlockSpec(shape, lambda *_: zeros)


def _mod_kernel(c_ref, w_ref, b_ref, o_ref):
    c = c_ref[...]
    act = (c * jax.nn.sigmoid(c)).astype(BF16)
    o_ref[0] = jnp.dot(act, w_ref[0].astype(BF16), preferred_element_type=F32) + b_ref[0]


def _mod_call(cvec, w_mod, b_mod):
    depth, d, n = w_mod.shape
    rows = cvec.shape[0]
    tn = MOD_COLS
    return pl.pallas_call(
        _mod_kernel,
        grid=(depth, n // tn),
        in_specs=[pl.BlockSpec((rows, d), lambda l, j: (0, 0)),
                  pl.BlockSpec((1, d, tn), lambda l, j: (l, 0, j)),
                  pl.BlockSpec((1, 1, tn), lambda l, j: (l, 0, j))],
        out_specs=pl.BlockSpec((1, rows, tn), lambda l, j: (l, 0, j)),
        out_shape=jax.ShapeDtypeStruct((depth, rows, n), F32),
        compiler_params=_cparams(2),
        name="adaln_mod",
    )(cvec, w_mod, b_mod.reshape(depth, 1, n))


def _inproj_kernel(x_ref, sh_ref, sc_ref, g1_ref, win_ref, gql_ref, wuq_ref, gkvl_ref, wk_ref,
                   wv_ref, ones_ref, gains_ref, ct_ref, st_ref,
                   lx_ref, lg_ref, pu_ref, q_ref, k_ref, v_ref, *, d_lru, q_rank, kv_rank, heads, rsub):
    lane = lax.broadcasted_iota(jnp.int32, (1, HEAD_PAD), 1)
    rot = (lane >= MLA_NOPE) & (lane < MLA_QK)
    gains = gains_ref[...]
    ones = ones_ref[...]
    inv_qk = 1.0 / MLA_QK
    n_main = heads * HEAD_PAD
    pair = 2 * HEAD_PAD
    extra = (lax.broadcasted_iota(jnp.int32, (BF16_ROWS, rsub), 0) == 0).astype(F32)

    for r0 in range(0, x_ref.shape[1], rsub):
        rows = slice(r0, r0 + rsub)
        x = x_ref[0, rows, :]
        xn = x * lax.rsqrt(jnp.mean(x * x, axis=-1, keepdims=True) + EPS)
        hx = (xn * g1_ref[...]) * (1.0 + sc_ref[0]) + sh_ref[0]
        z = jnp.dot(hx.astype(BF16), win_ref[...], preferred_element_type=F32)
        o = 0
        lx_ref[0, rows, :] = z[:, o:o + d_lru]; o += d_lru
        lg_ref[0, rows, :] = z[:, o:o + d_lru]; o += d_lru
        ql = z[:, o:o + q_rank]; o += q_rank
        kvl = z[:, o:o + kv_rank]; o += kv_rank
        krb = z[:, o:o + HEAD_PAD]; o += HEAD_PAD
        pu_ref[0, rows, :] = z[:, o:]

        qln = ql * lax.rsqrt(jnp.mean(ql * ql, axis=-1, keepdims=True) + EPS) * gql_ref[...]
        qf = jnp.dot(qln.astype(BF16), wuq_ref[...], preferred_element_type=F32)
        kvn = kvl * lax.rsqrt(jnp.mean(kvl * kvl, axis=-1, keepdims=True) + EPS) * gkvl_ref[...]
        kvn = kvn.astype(BF16)
        kf = jnp.dot(kvn, wk_ref[...], preferred_element_type=F32)
        vf = jnp.dot(kvn, wv_ref[...], preferred_element_type=F32)

        kr = jnp.where(rot, krb, 0.0)
        kr_sw = pltpu.roll(krb, HEAD_PAD - MLA_ROPE, axis=1)
        kr2 = jnp.concatenate([kr, kr], axis=1)
        ct, st = ct_ref[rows, :], st_ref[rows, :]
        aq, bq, ak, bk = ct * gains[0:1], st * gains[1:2], ct * gains[2:3], st * gains[3:4]
        for hp in range(heads // 2):
            qa = qf[:, hp * pair:(hp + 1) * pair]
            ss = jnp.dot((qa * qa).astype(BF16), ones, preferred_element_type=F32)
            rq = lax.rsqrt(ss * inv_qk + EPS)
            ka = kf[:, hp * pair:(hp + 1) * pair] + kr2
            ss = jnp.dot((ka * ka).astype(BF16), ones, preferred_element_type=F32)
            rk = lax.rsqrt(ss * inv_qk + EPS)
            for e in range(2):
                h = 2 * hp + e
                lo, hi = e * HEAD_PAD, (e + 1) * HEAD_PAD
                grp = qf[:, n_main + (h // 4) * HEAD_PAD:n_main + (h // 4 + 1) * HEAD_PAD]
                shift = (MLA_NOPE - (h % 4) * MLA_ROPE) % HEAD_PAD
                qb = pltpu.roll(grp, shift, axis=1) if shift else grp
                q_ref[0, h, rows, :] = (rq[:, lo:hi] * (qa[:, lo:hi] * aq + qb * bq)).astype(BF16)
                k_ref[0, h, rows, :] = (rk[:, lo:hi] * (ka[:, lo:hi] * ak + kr_sw * bk)).astype(BF16)
        for p in range(heads // 2):
            vt = vf[:, p * LANES:(p + 1) * LANES].T
            v_ref[0, p, :, rows] = jnp.concatenate(
                [vt[:MLA_V], extra, vt[MLA_V:], extra], axis=0).astype(BF16)


def _inproj_call(x, sh, sc, g1, lw, tables, tm):
    B, T, D = x.shape
    pos = pl.BlockSpec((tm, HEAD_PAD), lambda b, i: (i, 0))
    heads = lw["heads"]
    d_lru, q_rank, kv_rank = lw["d_lru"], lw["q_rank"], lw["kv_rank"]
    d_pool = lw["w_in"].shape[1] - (2 * d_lru + q_rank + kv_rank + HEAD_PAD)
    tok = lambda w: pl.BlockSpec((1, tm, w), lambda b, i: (b, i, 0))
    vec = pl.BlockSpec((1, 1, D), lambda b, i: (b, 0, 0))
    hd = lambda n: pl.BlockSpec((1, n, tm, LANES), lambda b, i: (b, 0, i, 0))
    kern = functools.partial(_inproj_kernel, d_lru=d_lru, q_rank=q_rank, kv_rank=kv_rank, heads=heads,
                             rsub=min(tm, ROW_BLOCK))
    return pl.pallas_call(
        kern,
        grid=(B, T // tm),
        in_specs=[tok(D), vec, vec, _const_spec((1, D)), _const_spec(lw["w_in"].shape),
                  _const_spec((1, q_rank)), _const_spec(lw["w_uq"].shape),
                  _const_spec((1, kv_rank)), _const_spec(lw["w_k"].shape), _const_spec(lw["w_v"].shape),
                  _const_spec(lw["head_ones"].shape), _const_spec(lw["head_gains"].shape), pos, pos],
        out_specs=[tok(d_lru), tok(d_lru), tok(d_pool), hd(heads), hd(heads),
                   pl.BlockSpec((1, heads // 2, 2 * V_ROWS, tm), lambda b, i: (b, 0, 0, i))],
        out_shape=[jax.ShapeDtypeStruct((B, T, d_lru), F32),
                   jax.ShapeDtypeStruct((B, T, d_lru), F32),
                   jax.ShapeDtypeStruct((B, T, d_pool), F32),
                   jax.ShapeDtypeStruct((B, heads, T, LANES), BF16),
                   jax.ShapeDtypeStruct((B, heads, T, LANES), BF16),
                   jax.ShapeDtypeStruct((B, heads // 2, 2 * V_ROWS, T), BF16)],
        compiler_params=_cparams(2),
        name="in_proj",
    )(x, sh, sc, g1, lw["w_in"], lw["g_q_lat"], lw["w_uq"], lw["g_kv_lat"], lw["w_k"], lw["w_v"],
      lw["head_ones"], lw["head_gains"], *tables)


def _roll_rows(a, shift):
    return pltpu.roll(a, shift % a.shape[0], axis=0)


def _scan8(a, b, reverse):
    n = a.shape[0]
    a3 = a.reshape(n // SUBLANES, SUBLANES, a.shape[1])
    b3 = b.reshape(n // SUBLANES, SUBLANES, b.shape[1])
    rid = lax.broadcasted_iota(jnp.int32, a3.shape, 1)
    for s in (1, 2, 4):
        if reverse:
            m = rid < SUBLANES - s
            sh = SUBLANES - s
        else:
            m = rid >= s
            sh = s
        a_sh = pltpu.roll(a3, sh, axis=1)
        b_sh = pltpu.roll(b3, sh, axis=1)
        b3 = jnp.where(m, a3 * b_sh + b3, b3)
        a3 = jnp.where(m, a3 * a_sh, a3)
    return a3, b3


def _chunk_scan(a, b, h, sa_ref, sb_ref, out_ref, row0, reverse):
    tc, C = a.shape
    rid = lax.broadcasted_iota(jnp.int32, (SUBLANES, LANES), 0)
    order = list(range(SCAN_GROUP - 1, -1, -1) if reverse else range(SCAN_GROUP))
    n_blocks = tc // SCAN_BLOCK
    blocks = range(n_blocks - 1, -1, -1) if reverse else range(n_blocks)
    for s in range(C // LANES):
        sa_ref[s] = a[:, s * LANES:(s + 1) * LANES]
        sb_ref[s] = b[:, s * LANES:(s + 1) * LANES]
    h_out = []
    for s in range(C // LANES):
        hs = h[:, s * LANES:(s + 1) * LANES]
        for k in blocks:
            base = k * SCAN_BLOCK
            av = [sa_ref[s, pl.ds(base + j, SUBLANES, stride=SCAN_GROUP), :] for j in range(SCAN_GROUP)]
            bv = [sb_ref[s, pl.ds(base + j, SUBLANES, stride=SCAN_GROUP), :] for j in range(SCAN_GROUP)]
            hh, pp = {}, {}
            prev = None
            for j in order:
                if prev is None:
                    hh[j], pp[j] = bv[j], av[j]
                else:
                    hh[j], pp[j] = av[j] * hh[prev] + bv[j], av[j] * pp[prev]
                prev = j
            pc, hc = _scan8(pp[prev], hh[prev], reverse)
            end = hc.reshape(SUBLANES, LANES) + pc.reshape(SUBLANES, LANES) * hs
            if reverse:
                enter = jnp.where(rid == SUBLANES - 1, hs, pltpu.roll(end, SUBLANES - 1, axis=0))
                edge = end[0:1]
            else:
                enter = jnp.where(rid == 0, hs, pltpu.roll(end, 1, axis=0))
                edge = end[SUBLANES - 1:SUBLANES]
            for j in range(SCAN_GROUP):
                out_ref[s, pl.ds(row0 + base + j, SUBLANES, stride=SCAN_GROUP), :] = hh[j] + pp[j] * enter
            hs = jnp.broadcast_to(edge, (SUBLANES, LANES))
        h_out.append(hs)
    return jnp.concatenate(h_out, axis=1)


def _seq_kernel(lx_ref, lg_ref, pu_ref, h0_ref, cw_ref, cb_ref, wg_ref, bg_ref, lam_ref, wp_ref,
                ps_ref, a_out_ref, p_out_ref, hfin_ref, xpad_ref, upad_ref, hf_ref, xc_ref,
                sa_ref, sb_ref, hb_ref, *, tc):
    T = lx_ref.shape[1]
    C = lx_ref.shape[2]
    nc = T // tc
    zeros_halo = jnp.zeros((HALO, C), F32)
    xpad_ref[0:HALO, :] = zeros_halo
    xpad_ref[HALO + T:HALO + T + HALO, :] = zeros_halo
    upad_ref[0:HALO, :] = zeros_halo
    upad_ref[HALO + T:HALO + T + HALO, :] = zeros_halo

    def fill(c, carry):
        r0 = pl.multiple_of(c * tc, tc)
        xpad_ref[pl.ds(HALO + r0, tc), :] = lx_ref[0, pl.ds(r0, tc), :]
        upad_ref[pl.ds(HALO + r0, tc), :] = pu_ref[0, pl.ds(r0, tc), :]
        return carry

    lax.fori_loop(0, nc, fill, 0)

    lam = lam_ref[...]
    coef = -LRU_C * (jnp.maximum(-lam, 0.0) + jnp.log(1.0 + jnp.exp(-jnp.abs(lam))))
    cw = cw_ref[...]
    cb = cb_ref[...]
    bg = bg_ref[...]

    def conv_chunk(r0):
        w = xpad_ref[pl.ds(r0 + HALO - SUBLANES, tc + 2 * SUBLANES), :]
        y = (cw[0:1] * _roll_rows(w, 1) + cw[1:2] * w + cw[2:3] * _roll_rows(w, -1)
             + cw[3:4] * _roll_rows(w, -2))
        return y[SUBLANES:SUBLANES + tc] + cb

    def coeffs(xc, d):
        g = jnp.dot(xc.astype(BF16), wg_ref[:, 2 * C * d:2 * C * (d + 1)],
                    preferred_element_type=F32) + bg[:, 2 * C * d:2 * C * (d + 1)]
        r = jax.nn.sigmoid(g[:, :C])
        i = jax.nn.sigmoid(g[:, C:])
        log_a = coef[d:d + 1] * r
        a = jnp.exp(log_a)
        b = jnp.exp2(0.5 * jnp.log2(1.0 - a * a)) * (i * xc)
        return a, b

    lane = lax.broadcasted_iota(jnp.int32, (1, C), 1)
    gdim = C // len(POOL_WINDOWS)

    inv_win = None
    for g, win in enumerate(POOL_WINDOWS):
        inv_g = jnp.full((1, C), 1.0 / win, F32)
        inv_win = inv_g if inv_win is None else jnp.where(lane >= g * gdim, inv_g, inv_win)

    def pool_rows(r0, n, edge):
        w = upad_ref[pl.ds(r0, n + 2 * HALO), :]
        s2 = w + _roll_rows(w, 1)
        s4 = _roll_rows(s2, 1) + _roll_rows(s2, -1)
        s8 = _roll_rows(s4, 2) + _roll_rows(s4, -2)
        s16 = _roll_rows(s8, 4) + _roll_rows(s8, -4)
        sel = s2[HALO:HALO + n]
        for g, s in enumerate((s4, s8, s16), start=1):
            sel = jnp.where(lane >= g * gdim, s[HALO:HALO + n], sel)
        if edge:
            t = r0 + lax.broadcasted_iota(jnp.int32, (n, 1), 0)
            cnt = None
            for g, win in enumerate(POOL_WINDOWS):
                cg = (jnp.minimum(t - win // 2 + win, T) - jnp.maximum(t - win // 2, 0)).astype(F32)
                cg = jnp.broadcast_to(cg, (n, C))
                cnt = cg if cnt is None else jnp.where(lane >= g * gdim, cg, cnt)
            mean = sel / cnt
        else:
            mean = sel * inv_win
        mixed = mean - w[HALO:HALO + n]
        y = jnp.dot(mixed.astype(BF16), wp_ref[...], preferred_element_type=F32) * ps_ref[...]
        p_out_ref[0, pl.ds(r0, n), :] = y.astype(p_out_ref.dtype)

    per = 2 if nc % 2 == 0 else 1
    n_slab = C // LANES

    def fwd(i, h):
        for par in range(per):
            r0 = pl.multiple_of((per * i + par) * tc, tc)
            xc = conv_chunk(r0)
            xc_ref[pl.ds(r0, tc), :] = xc
            a, b = coeffs(xc, 0)
            h = _chunk_scan(a, b, h, sa_ref.at[par], sb_ref.at[par], hf_ref, r0, False)
        return h

    h = lax.fori_loop(0, nc // per, fwd, jnp.broadcast_to(h0_ref[0, 0:1, :], (SUBLANES, C)))
    hfin_ref[0, 0:1, :] = h[0:1]

    def bwd(i, h):
        for par in range(per):
            r0 = pl.multiple_of((nc - 1 - (per * i + par)) * tc, tc)
            rows = pl.ds(r0, tc)
            pool_rows(r0, tc, False)
            a, b = coeffs(xc_ref[rows, :], 1)
            h = _chunk_scan(a, b, h, sa_ref.at[par], sb_ref.at[par], hb_ref.at[par], 0, True)
            y = (jnp.concatenate([hb_ref[par, s] for s in range(n_slab)], axis=1)
                 + jnp.concatenate([hf_ref[s, rows, :] for s in range(n_slab)], axis=1))
            a_out_ref[0, rows, :] = (y * jax.nn.gelu(lg_ref[0, rows, :])).astype(a_out_ref.dtype)
        return h

    h = lax.fori_loop(0, nc // per, bwd, jnp.broadcast_to(h0_ref[0, 1:2, :], (SUBLANES, C)))
    hfin_ref[0, 1:2, :] = h[0:1]

    pool_rows(0, BF16_ROWS, True)
    pool_rows(T - BF16_ROWS, BF16_ROWS, True)


def _seq_call(lx, lg, pu, h0, lw, tc):
    B, T, C = lx.shape
    seq = pl.BlockSpec((1, T, C), lambda b: (b, 0, 0))
    st = pl.BlockSpec((1, 2, C), lambda b: (b, 0, 0))
    return pl.pallas_call(
        functools.partial(_seq_kernel, tc=tc),
        grid=(B,),
        in_specs=[seq, seq, seq, st, _const_spec((CONV_W, C)), _const_spec((1, C)),
                  _const_spec((C, 4 * C)), _const_spec((1, 4 * C)), _const_spec((2, C)),
                  _const_spec((C, C)), _const_spec((1, C))],
        out_specs=[seq, seq, st],
        out_shape=[jax.ShapeDtypeStruct((B, T, C), BF16), jax.ShapeDtypeStruct((B, T, C), BF16),
                   jax.ShapeDtypeStruct((B, 2, C), F32)],
        scratch_shapes=[pltpu.VMEM((T + 2 * HALO, C), F32), pltpu.VMEM((T + 2 * HALO, C), F32),
                        pltpu.VMEM((C // LANES, T, LANES), F32), pltpu.VMEM((T, C), F32),
                        pltpu.VMEM((2, C // LANES, tc, LANES), F32),
                        pltpu.VMEM((2, C // LANES, tc, LANES), F32),
                        pltpu.VMEM((2, C // LANES, tc, LANES), F32)],
        compiler_params=_cparams(1),
        name="seq_mix",
    )(lx, lg, pu, h0, lw["conv_w"], lw["conv_b"], lw["w_gate"], lw["b_gate"], lw["lam"],
      lw["w_pool"], lw["pool_scale"])


def _attn_kernel(*refs, n_seg, tq_sub, kc):
    q_ref = refs[0]
    kv_refs = refs[1:1 + 2 * n_seg]
    o_ref = refs[1 + 2 * n_seg]
    s_scr = refs[2 + 2 * n_seg]
    n_sub = q_ref.shape[2] // tq_sub
    units = [(hh, j) for j in range(n_sub) for hh in range(2)]
    chunks = []
    off = 0
    for seg in range(n_seg):
        lk = kv_refs[2 * seg].shape[2]
        for st in range(0, lk, kc):
            sz = min(kc, lk - st)
            chunks.append((seg, st, sz, off))
            off += sz
    nch = len(chunks)
    mvec = {}
    col_max = {}
    acc = {}
    results = {}
    pv_hist = []

    def score_chunk(u, c):
        hh, j = units[u]
        seg, st, sz, off = chunks[c]
        q = q_ref[0, hh, j * tq_sub:(j + 1) * tq_sub, :]
        k = kv_refs[2 * seg][0, hh, st:st + sz, :]
        s = lax.dot_general(k, q, (((1,), (1,)), ((), ())), preferred_element_type=F32)
        s_scr[u % 2, off:off + sz, :] = s
        cm = jnp.max(s.reshape(sz // SUBLANES, SUBLANES, tq_sub), axis=0)
        mvec[u] = cm if c == 0 else jnp.maximum(mvec[u], cm)
        if c == nch - 1:
            col_max[u] = jnp.max(mvec[u], axis=0, keepdims=True)

    def weight_chunk(u, c):
        hh, j = units[u]
        seg, st, sz, off = chunks[c]
        m = col_max[u]
        if len(pv_hist) >= ORDER_DEPTH:
            bits = pltpu.bitcast(pv_hist[-ORDER_DEPTH][MLA_V:MLA_V + 1], jnp.uint32)
            m = m + ((bits >> 16) >> 16).astype(F32)
        p = jnp.exp2((s_scr[u % 2, off:off + sz, :] - m).astype(BF16))
        vt = kv_refs[2 * seg + 1][0, 0, hh * V_ROWS:(hh + 1) * V_ROWS, st:st + sz]
        pv = jnp.dot(vt, p, preferred_element_type=F32)
        pv_hist.append(pv)
        acc[u] = pv if c == 0 else acc[u] + pv
        if c == nch - 1:
            results[units[u]] = acc[u][:MLA_V] / acc[u][MLA_V:MLA_V + 1]

    lead = nch + min(SCORE_LEAD, nch - 1)
    events = []
    for u in range(len(units)):
        for c in range(nch):
            events.append((u * nch + c - lead, 0, u, c))
            events.append((u * nch + c, 1, u, c))
    for _, kind, u, c in sorted(events):
        (weight_chunk if kind else score_chunk)(u, c)

    for j in range(n_sub):
        o_t = jnp.concatenate([results[(0, j)], results[(1, j)]], axis=0)
        o_ref[0, j * tq_sub:(j + 1) * tq_sub, :] = o_t.T.astype(o_ref.dtype)


def _attn_call(q, segs, tq):
    B, H, T, _ = q.shape
    tq_sub = min(tq, ATTN_Q_UNIT)
    in_specs = [pl.BlockSpec((1, 2, tq, LANES), lambda b, p, i: (b, p, i, 0))]
    args = [q]
    lk_total = 0
    for k, v in segs:
        lk = k.shape[2]
        lk_total += lk
        in_specs.append(pl.BlockSpec((1, 2, lk, LANES), lambda b, p, i: (b, p, 0, 0)))
        in_specs.append(pl.BlockSpec((1, 1, 2 * V_ROWS, lk), lambda b, p, i: (b, p, 0, 0)))
        args += [k, v]
    return pl.pallas_call(
        functools.partial(_attn_kernel, n_seg=len(segs), tq_sub=tq_sub, kc=ATTN_KEY_CHUNK),
        grid=(B, H // 2, T // tq),
        in_specs=in_specs,
        out_specs=pl.BlockSpec((1, tq, LANES), lambda b, p, i: (b, i, p)),
        out_shape=jax.ShapeDtypeStruct((B, T, H * MLA_V), BF16),
        scratch_shapes=[pltpu.VMEM((2, lk_total, tq_sub), F32)],
        compiler_params=_cparams(3),
        name="attention",
    )(*args)


def _post_kernel(x_ref, a_ref, att_ref, p_ref, g1_ref, sh_ref, sc_ref, g2_ref, gn_ref, wo_ref,
                 w1_ref, w2_ref, o_ref, *, ff_chunk):
    d_lru = a_ref.shape[2]
    d_att = att_ref.shape[2]
    y = jnp.dot(a_ref[0], wo_ref[0:d_lru, :], preferred_element_type=F32)
    y += jnp.dot(att_ref[0], wo_ref[d_lru:d_lru + d_att, :], preferred_element_type=F32)
    y += jnp.dot(p_ref[0], wo_ref[d_lru + d_att:, :], preferred_element_type=F32)
    x1 = x_ref[0] + g1_ref[0] * y
    xn = x1 * lax.rsqrt(jnp.mean(x1 * x1, axis=-1, keepdims=True) + EPS)
    h = ((xn * gn_ref[...]) * (1.0 + sc_ref[0]) + sh_ref[0]).astype(BF16)
    d_ff = w1_ref.shape[1]
    acc = None
    for c in range(d_ff // ff_chunk):
        u = jnp.dot(h, w1_ref[:, c * ff_chunk:(c + 1) * ff_chunk], preferred_element_type=F32)
        u = jnp.square(jnp.maximum(u, 0.0)).astype(BF16)
        part = jnp.dot(u, w2_ref[c * ff_chunk:(c + 1) * ff_chunk, :], preferred_element_type=F32)
        acc = part if acc is None else acc + part
    o_ref[0] = x1 + g2_ref[0] * acc


def _post_call(x, a, att, p, g1, sh, sc, g2, gn, lw, tm):
    B, T, D = x.shape
    tok = lambda w: pl.BlockSpec((1, tm, w), lambda b, i: (b, i, 0))
    vec = pl.BlockSpec((1, 1, D), lambda b, i: (b, 0, 0))
    single = lambda shape: pl.BlockSpec(shape, lambda b, i: (0,) * len(shape),
                                        pipeline_mode=pl.Buffered(1))
    return pl.pallas_call(
        functools.partial(_post_kernel, ff_chunk=FF_CHUNK),
        grid=(B, T // tm),
        in_specs=[tok(D), tok(a.shape[2]), tok(att.shape[2]), tok(p.shape[2]), vec, vec, vec, vec,
                  _const_spec((1, D)), single(lw["w_out"].shape), single(lw["w_ff1"].shape),
                  single(lw["w_ff2"].shape)],
        out_specs=tok(D),
        out_shape=jax.ShapeDtypeStruct((B, T, D), F32),
        compiler_params=_cparams(2),
        name="out_proj_mlp",
    )(x, a, att, p, g1, sh, sc, g2, gn, lw["w_out"], lw["w_ff1"], lw["w_ff2"])


def _rope_partner():
    j = np.arange(MLA_ROPE)
    half = (j % (2 * ROPE_FREQS)) // ROPE_FREQS
    return np.where(half == 0, j + ROPE_FREQS, j - ROPE_FREQS)


def _block_diag(w):
    n, a, b = w.shape
    eye = jnp.asarray(np.eye(n, dtype=np.float32), w.dtype)
    return (w[:, :, None, :] * eye[:, None, :, None]).reshape(n * a, n * b)


def _layer_weights(l, w_in, conv_w, conv_b, lru_w_a, lru_b_a, lru_w_x, lru_b_x, lru_lambda, g_q_lat,
                   w_uq, g_kv_lat, w_ukv, g_qn, g_kn, w_pool, pool_scale, w_out, w_ff1, w_ff2):
    d = w_in.shape[1]
    d_lru = conv_w.shape[2]
    q_rank = w_uq.shape[1]
    kv_rank = w_ukv.shape[1]
    heads = w_uq.shape[2] // MLA_QK
    partner = _rope_partner()
    o_kr = 2 * d_lru + q_rank + kv_rank
    wi = w_in[l]
    kr = wi[:, o_kr:o_kr + MLA_ROPE]
    krb = jnp.concatenate([jnp.zeros((d, MLA_NOPE), F32), kr, kr[:, partner]], axis=1)
    w_in_p = jnp.concatenate([wi[:, :o_kr], krb, wi[:, o_kr + MLA_ROPE:]], axis=1).astype(BF16)
    wq = w_uq[l].reshape(q_rank, heads, MLA_QK)
    pad = jnp.zeros((q_rank, heads, HEAD_PAD - MLA_QK), F32)
    wq_main = jnp.concatenate([wq, pad], axis=2)
    wq_partner = wq[:, :, MLA_NOPE + partner]
    wq = jnp.concatenate([wq_main.reshape(q_rank, heads * HEAD_PAD),
                          wq_partner.reshape(q_rank, heads * MLA_ROPE)], axis=1)
    wkv = w_ukv[l].reshape(kv_rank, heads, MLA_NOPE + MLA_V)
    wk = jnp.concatenate([wkv[:, :, :MLA_NOPE], jnp.zeros((kv_rank, heads, HEAD_PAD - MLA_NOPE), F32)],
                         axis=2).reshape(kv_rank, heads * HEAD_PAD)
    wv = wkv[:, :, MLA_NOPE:].reshape(kv_rank, heads * MLA_V)
    zpad = jnp.zeros((HEAD_PAD - MLA_QK,), F32)
    znope = jnp.zeros((MLA_NOPE,), F32)
    q_scale = LOG2E / math.sqrt(MLA_QK)
    gq = jnp.concatenate([g_qn[l], zpad]) * q_scale
    gq_sw = jnp.concatenate([znope, g_qn[l][MLA_NOPE + partner], zpad]) * q_scale
    gk = jnp.concatenate([g_kn[l], zpad])
    gk_sw = jnp.concatenate([znope, g_kn[l][MLA_NOPE + partner], zpad])
    head_ones = jnp.asarray(np.kron(np.eye(2), np.ones((HEAD_PAD, HEAD_PAD))), BF16)
    w_gate = jnp.concatenate([_block_diag(lru_w_a[l, 0]), _block_diag(lru_w_x[l, 0]),
                              _block_diag(lru_w_a[l, 1]), _block_diag(lru_w_x[l, 1])], axis=1)
    b_gate = jnp.concatenate([lru_b_a[l, 0], lru_b_x[l, 0], lru_b_a[l, 1], lru_b_x[l, 1]])
    return dict(
        heads=heads, d_lru=d_lru, q_rank=q_rank, kv_rank=kv_rank,
        w_in=w_in_p, g_q_lat=g_q_lat[l][None], w_uq=wq.astype(BF16), g_kv_lat=g_kv_lat[l][None],
        w_k=wk.astype(BF16), w_v=wv.astype(BF16), head_gains=jnp.stack([gq, gq_sw, gk, gk_sw]),
        head_ones=head_ones,
        conv_w=conv_w[l], conv_b=conv_b[l][None], w_gate=w_gate.astype(BF16), b_gate=b_gate[None],
        lam=lru_lambda[l], w_pool=_block_diag(w_pool[l]).astype(BF16), pool_scale=pool_scale[l][None],
        w_out=w_out[l].astype(BF16), w_ff1=w_ff1[l].astype(BF16), w_ff2=w_ff2[l].astype(BF16))


def _rope_tables(T):
    f32 = np.float32
    j = np.arange(MLA_ROPE)
    axis = j // (2 * ROPE_FREQS)
    half = (j % (2 * ROPE_FREQS)) // ROPE_FREQS
    f = j % ROPE_FREQS
    t = np.arange(T)
    pos = np.stack([(t // GRID_W).astype(f32), (t % GRID_W).astype(f32)], axis=-1)
    freqs = np.power(f32(ROPE_BASE), -np.arange(ROPE_FREQS, dtype=f32) / f32(ROPE_FREQS)).astype(f32)
    ang = (pos[:, axis] * freqs[f]).astype(f32)
    sign = np.where(half == 0, -1.0, 1.0).astype(f32)
    ct = np.concatenate([np.ones((T, MLA_NOPE), f32), np.cos(ang).astype(f32),
                         np.zeros((T, HEAD_PAD - MLA_QK), f32)], axis=1)
    st = np.concatenate([np.zeros((T, MLA_NOPE), f32), (np.sin(ang) * sign).astype(f32),
                         np.zeros((T, HEAD_PAD - MLA_QK), f32)], axis=1)
    return jnp.asarray(ct), jnp.asarray(st)


def _no_rope_tables(T):
    lane = np.arange(HEAD_PAD)
    ct = np.broadcast_to((lane < MLA_QK).astype(np.float32), (T, HEAD_PAD))
    return jnp.asarray(ct), jnp.zeros((T, HEAD_PAD), F32)


def _tile(n, pref):
    t = min(n, pref)
    while n % t:
        t //= 2
    return t


def kernel(x, c, ctx, c_ctx, w_mod, b_mod, g_norm1, g_norm2, w_in, conv_w, conv_b, lru_w_a, lru_b_a,
           lru_w_x, lru_b_x, lru_lambda, g_q_lat, w_uq, g_kv_lat, w_ukv, g_qn, g_kn, w_pool, pool_scale,
           w_out, w_ff1, w_ff2):
    B, L, D = x.shape
    Lc = ctx.shape[1]
    depth = w_mod.shape[0]
    d_lru = conv_w.shape[2]

    rows = -(-(B + 1) // SUBLANES) * SUBLANES
    cvec = jnp.concatenate([c, c_ctx[None], jnp.zeros((rows - B - 1, D), F32)], axis=0)
    mods = _mod_call(cvec, w_mod, b_mod)

    rope_l = _rope_tables(L)
    rope_c = _no_rope_tables(Lc)
    tm_l, tm_c = _tile(L, TOKEN_STEP), _tile(Lc, TOKEN_STEP)
    tp_l, tp_c = _tile(L, TOKEN_STEP), _tile(Lc, TOKEN_STEP)
    tq_l, tq_c = _tile(L, ATTN_Q_STEP), _tile(Lc, ATTN_Q_STEP)
    tc_l, tc_c = _tile(L, ROW_BLOCK), _tile(Lc, ROW_BLOCK)

    h = ctx
    zeros_state = jnp.zeros((B, 2, d_lru), F32)
    for l in range(depth):
        last = l == depth - 1
        lw = _layer_weights(l, w_in, conv_w, conv_b, lru_w_a, lru_b_a, lru_w_x, lru_b_x, lru_lambda,
                            g_q_lat, w_uq, g_kv_lat, w_ukv, g_qn, g_kn, w_pool, pool_scale, w_out,
                            w_ff1, w_ff2)
        ml = mods[l, :B].reshape(B, 1, 6, D)
        mc = jnp.broadcast_to(mods[l, B].reshape(1, 1, 6, D), (B, 1, 6, D))
        sh1, sc1, g1, sh2, sc2, g2 = (ml[:, :, i] for i in range(6))
        csh1, csc1, cg1, csh2, csc2, cg2 = (mc[:, :, i] for i in range(6))
        gn1 = g_norm1[l][None]
        gn2 = g_norm2[l][None]

        lx_c, lg_c, pu_c, q_c, k_c, v_c = _inproj_call(h, csh1, csc1, gn1, lw, rope_c, tm_c)
        a_c, p_c, hfin = _seq_call(lx_c, lg_c, pu_c, zeros_state, lw, tc_c)
        lx_l, lg_l, pu_l, q_l, k_l, v_l = _inproj_call(x, sh1, sc1, gn1, lw, rope_l, tm_l)
        a_l, p_l, _ = _seq_call(lx_l, lg_l, pu_l, hfin, lw, tc_l)
        att_l = _attn_call(q_l, [(k_c, v_c), (k_l, v_l)], tq_l)
        x = _post_call(x, a_l, att_l, p_l, g1, sh2, sc2, g2, gn2, lw, tp_l)
        if not last:
            att_c = _attn_call(q_c, [(k_c, v_c)], tq_c)
            h = _post_call(h, a_c, att_c, p_c, cg1, csh2, csc2, cg2, gn2, lw, tp_c)
    return x
```

```python
import functools
import math

import jax
import jax.numpy as jnp
import numpy as np
from jax import lax
from jax.experimental import pallas as pl
from jax.experimental.pallas import tpu as pltpu

F32 = jnp.float32
BF16 = jnp.bfloat16

EPS = 1e-6
GRID_W = 64
LRU_HEADS = 4
LRU_C = 8.0
CONV_W = 4
MLA_V = 64
MLA_NOPE = 64
MLA_ROPE = 32
MLA_QK = MLA_NOPE + MLA_ROPE
ROPE_FREQS = MLA_ROPE // 4
ROPE_BASE = 10000.0
POOL_WINDOWS = (2, 4, 8, 16)

LANES = 128
SUBLANES = 8
MXU_DIM = 256
ROW_BLOCK = MXU_DIM
ATTN_Q_UNIT = MXU_DIM
ATTN_KEY_CHUNK = MXU_DIM
ATTN_Q_STEP = 4096
TOKEN_STEP = 1024
FF_CHUNK = 1024
MOD_COLS = 1024
SCORE_LEAD = 4
ORDER_DEPTH = 5
HEAD_PAD = LANES
BF16_ROWS = 16
V_ROWS = MLA_V + BF16_ROWS
HALO = 16
SCAN_GROUP = 4
SCAN_BLOCK = SCAN_GROUP * SUBLANES
VMEM_LIMIT = 56 * 1024 * 1024

LOG2E = 1.4426950408889634


def _cparams(n_grid, flags=None):
    return pltpu.CompilerParams(dimension_semantics=("arbitrary",) * n_grid,
                                vmem_limit_bytes=VMEM_LIMIT, flags=flags)


def _const_spec(shape):
    zeros = (0,) * len(shape)
    return pl.BlockSpec(shape, lambda *_: zeros)


def _mod_kernel(c_ref, w_ref, b_ref, o_ref):
    c = c_ref[...]
    act = (c * jax.nn.sigmoid(c)).astype(BF16)
    o_ref[0] = jnp.dot(act, w_ref[0].astype(BF16), preferred_element_type=F32) + b_ref[0]


def _mod_call(cvec, w_mod, b_mod):
    depth, d, n = w_mod.shape
    rows = cvec.shape[0]
    tn = MOD_COLS
    return pl.pallas_call(
        _mod_kernel,
        grid=(depth, n // tn),
        in_specs=[pl.BlockSpec((rows, d), lambda l, j: (0, 0)),
                  pl.BlockSpec((1, d, tn), lambda l, j: (l, 0, j)),
                  pl.BlockSpec((1, 1, tn), lambda l, j: (l, 0, j))],
        out_specs=pl.BlockSpec((1, rows, tn), lambda l, j: (l, 0, j)),
        out_shape=jax.ShapeDtypeStruct((depth, rows, n), F32),
        compiler_params=_cparams(2),
        name="adaln_mod",
    )(cvec, w_mod, b_mod.reshape(depth, 1, n))


def _inproj_kernel(x_ref, sh_ref, sc_ref, g1_ref, win_ref, gql_ref, wuq_ref, gkvl_ref, wk_ref,
                   wv_ref, ones_ref, gains_ref, ct_ref, st_ref,
                   lx_ref, lg_ref, pu_ref, q_ref, k_ref, v_ref, *, d_lru, q_rank, kv_rank, heads, rsub):
    lane = lax.broadcasted_iota(jnp.int32, (1, HEAD_PAD), 1)
    rot = (lane >= MLA_NOPE) & (lane < MLA_QK)
    gains = gains_ref[...]
    ones = ones_ref[...]
    inv_qk = 1.0 / MLA_QK
    n_main = heads * HEAD_PAD
    pair = 2 * HEAD_PAD
    extra = (lax.broadcasted_iota(jnp.int32, (BF16_ROWS, rsub), 0) == 0).astype(F32)

    for r0 in range(0, x_ref.shape[1], rsub):
        rows = slice(r0, r0 + rsub)
        x = x_ref[0, rows, :]
        xn = x * lax.rsqrt(jnp.mean(x * x, axis=-1, keepdims=True) + EPS)
        hx = (xn * g1_ref[...]) * (1.0 + sc_ref[0]) + sh_ref[0]
        z = jnp.dot(hx.astype(BF16), win_ref[...], preferred_element_type=F32)
        o = 0
        lx_ref[0, rows, :] = z[:, o:o + d_lru]; o += d_lru
        lg_ref[0, rows, :] = z[:, o:o + d_lru]; o += d_lru
        ql = z[:, o:o + q_rank]; o += q_rank
        kvl = z[:, o:o + kv_rank]; o += kv_rank
        krb = z[:, o:o + HEAD_PAD]; o += HEAD_PAD
        pu_ref[0, rows, :] = z[:, o:]

        qln = ql * lax.rsqrt(jnp.mean(ql * ql, axis=-1, keepdims=True) + EPS) * gql_ref[...]
        qln = qln.astype(BF16)
        kvn = kvl * lax.rsqrt(jnp.mean(kvl * kvl, axis=-1, keepdims=True) + EPS) * gkvl_ref[...]
        kvn = kvn.astype(BF16)
        qp = jnp.dot(qln, wuq_ref[:, n_main:], preferred_element_type=F32)
        vf = jnp.dot(kvn, wv_ref[...], preferred_element_type=F32)

        kr = jnp.where(rot, krb, 0.0)
        kr_sw = pltpu.roll(krb, HEAD_PAD - MLA_ROPE, axis=1)
        kr2 = jnp.concatenate([kr, kr], axis=1)
        ct, st = ct_ref[rows, :], st_ref[rows, :]
        aq, bq, ak, bk = ct * gains[0:1], st * gains[1:2], ct * gains[2:3], st * gains[3:4]
        for hp in range(heads // 2):
            qa = jnp.dot(qln, wuq_ref[:, hp * pair:(hp + 1) * pair], preferred_element_type=F32)
            ss = jnp.dot((qa * qa).astype(BF16), ones, preferred_element_type=F32)
            rq = lax.rsqrt(ss * inv_qk + EPS)
            ka = jnp.dot(kvn, wk_ref[:, hp * pair:(hp + 1) * pair], preferred_element_type=F32) + kr2
            ss = jnp.dot((ka * ka).astype(BF16), ones, preferred_element_type=F32)
            rk = lax.rsqrt(ss * inv_qk + EPS)
            for e in range(2):
                h = 2 * hp + e
                lo, hi = e * HEAD_PAD, (e + 1) * HEAD_PAD
                grp = qp[:, (h // 4) * HEAD_PAD:(h // 4 + 1) * HEAD_PAD]
                shift = (MLA_NOPE - (h % 4) * MLA_ROPE) % HEAD_PAD
                qb = pltpu.roll(grp, shift, axis=1) if shift else grp
                q_ref[0, h, rows, :] = (rq[:, lo:hi] * (qa[:, lo:hi] * aq + qb * bq)).astype(BF16)
                k_ref[0, h, rows, :] = (rk[:, lo:hi] * (ka[:, lo:hi] * ak + kr_sw * bk)).astype(BF16)
        for p in range(heads // 2):
            vt = vf[:, p * LANES:(p + 1) * LANES].T
            v_ref[0, p, :, rows] = jnp.concatenate(
                [vt[:MLA_V], extra, vt[MLA_V:], extra], axis=0).astype(BF16)


def _inproj_call(x, sh, sc, g1, lw, tables, tm):
    B, T, D = x.shape
    pos = pl.BlockSpec((tm, HEAD_PAD), lambda b, i: (i, 0))
    heads = lw["heads"]
    d_lru, q_rank, kv_rank = lw["d_lru"], lw["q_rank"], lw["kv_rank"]
    d_pool = lw["w_in"].shape[1] - (2 * d_lru + q_rank + kv_rank + HEAD_PAD)
    tok = lambda w: pl.BlockSpec((1, tm, w), lambda b, i: (b, i, 0))
    vec = pl.BlockSpec((1, 1, D), lambda b, i: (b, 0, 0))
    hd = lambda n: pl.BlockSpec((1, n, tm, LANES), lambda b, i: (b, 0, i, 0))
    kern = functools.partial(_inproj_kernel, d_lru=d_lru, q_rank=q_rank, kv_rank=kv_rank, heads=heads,
                             rsub=min(tm, ROW_BLOCK))
    return pl.pallas_call(
        kern,
        grid=(B, T // tm),
        in_specs=[tok(D), vec, vec, _const_spec((1, D)), _const_spec(lw["w_in"].shape),
                  _const_spec((1, q_rank)), _const_spec(lw["w_uq"].shape),
                  _const_spec((1, kv_rank)), _const_spec(lw["w_k"].shape), _const_spec(lw["w_v"].shape),
                  _const_spec(lw["head_ones"].shape), _const_spec(lw["head_gains"].shape), pos, pos],
        out_specs=[tok(d_lru), tok(d_lru), tok(d_pool), hd(heads), hd(heads),
                   pl.BlockSpec((1, heads // 2, 2 * V_ROWS, tm), lambda b, i: (b, 0, 0, i))],
        out_shape=[jax.ShapeDtypeStruct((B, T, d_lru), F32),
                   jax.ShapeDtypeStruct((B, T, d_lru), F32),
                   jax.ShapeDtypeStruct((B, T, d_pool), F32),
                   jax.ShapeDtypeStruct((B, heads, T, LANES), BF16),
                   jax.ShapeDtypeStruct((B, heads, T, LANES), BF16),
                   jax.ShapeDtypeStruct((B, heads // 2, 2 * V_ROWS, T), BF16)],
        compiler_params=_cparams(2),
        name="in_proj",
    )(x, sh, sc, g1, lw["w_in"], lw["g_q_lat"], lw["w_uq"], lw["g_kv_lat"], lw["w_k"], lw["w_v"],
      lw["head_ones"], lw["head_gains"], *tables)


def _roll_rows(a, shift):
    return pltpu.roll(a, shift % a.shape[0], axis=0)


def _scan8(a, b, reverse):
    n = a.shape[0]
    a3 = a.reshape(n // SUBLANES, SUBLANES, a.shape[1])
    b3 = b.reshape(n // SUBLANES, SUBLANES, b.shape[1])
    rid = lax.broadcasted_iota(jnp.int32, a3.shape, 1)
    for s in (1, 2, 4):
        if reverse:
            m = rid < SUBLANES - s
            sh = SUBLANES - s
        else:
            m = rid >= s
            sh = s
        a_sh = pltpu.roll(a3, sh, axis=1)
        b_sh = pltpu.roll(b3, sh, axis=1)
        b3 = jnp.where(m, a3 * b_sh + b3, b3)
        a3 = jnp.where(m, a3 * a_sh, a3)
    return a3, b3


def _chunk_scan(a, b, h, sa_ref, sb_ref, out_ref, row0, reverse):
    tc, C = a.shape
    rid = lax.broadcasted_iota(jnp.int32, (SUBLANES, LANES), 0)
    order = list(range(SCAN_GROUP - 1, -1, -1) if reverse else range(SCAN_GROUP))
    n_blocks = tc // SCAN_BLOCK
    blocks = range(n_blocks - 1, -1, -1) if reverse else range(n_blocks)
    for s in range(C // LANES):
        sa_ref[s] = a[:, s * LANES:(s + 1) * LANES]
        sb_ref[s] = b[:, s * LANES:(s + 1) * LANES]
    h_out = []
    for s in range(C // LANES):
        hs = h[:, s * LANES:(s + 1) * LANES]
        for k in blocks:
            base = k * SCAN_BLOCK
            av = [sa_ref[s, pl.ds(base + j, SUBLANES, stride=SCAN_GROUP), :] for j in range(SCAN_GROUP)]
            bv = [sb_ref[s, pl.ds(base + j, SUBLANES, stride=SCAN_GROUP), :] for j in range(SCAN_GROUP)]
            hh, pp = {}, {}
            prev = None
            for j in order:
                if prev is None:
                    hh[j], pp[j] = bv[j], av[j]
                else:
                    hh[j], pp[j] = av[j] * hh[prev] + bv[j], av[j] * pp[prev]
                prev = j
            pc, hc = _scan8(pp[prev], hh[prev], reverse)
            end = hc.reshape(SUBLANES, LANES) + pc.reshape(SUBLANES, LANES) * hs
            if reverse:
                enter = jnp.where(rid == SUBLANES - 1, hs, pltpu.roll(end, SUBLANES - 1, axis=0))
                edge = end[0:1]
            else:
                enter = jnp.where(rid == 0, hs, pltpu.roll(end, 1, axis=0))
                edge = end[SUBLANES - 1:SUBLANES]
            for j in range(SCAN_GROUP):
                out_ref[s, pl.ds(row0 + base + j, SUBLANES, stride=SCAN_GROUP), :] = hh[j] + pp[j] * enter
            hs = jnp.broadcast_to(edge, (SUBLANES, LANES))
        h_out.append(hs)
    return jnp.concatenate(h_out, axis=1)


def _seq_kernel(lx_ref, lg_ref, pu_ref, h0_ref, cw_ref, cb_ref, wg_ref, bg_ref, lam_ref, wp_ref,
                ps_ref, a_out_ref, p_out_ref, hfin_ref, xpad_ref, upad_ref, hf_ref, xc_ref,
                sa_ref, sb_ref, hb_ref, *, tc):
    T = lx_ref.shape[1]
    C = lx_ref.shape[2]
    nc = T // tc
    zeros_halo = jnp.zeros((HALO, C), F32)
    xpad_ref[0:HALO, :] = zeros_halo
    xpad_ref[HALO + T:HALO + T + HALO, :] = zeros_halo
    upad_ref[0:HALO, :] = zeros_halo
    upad_ref[HALO + T:HALO + T + HALO, :] = zeros_halo

    def fill(c, carry):
        r0 = pl.multiple_of(c * tc, tc)
        xpad_ref[pl.ds(HALO + r0, tc), :] = lx_ref[0, pl.ds(r0, tc), :]
        upad_ref[pl.ds(HALO + r0, tc), :] = pu_ref[0, pl.ds(r0, tc), :]
        return carry

    lax.fori_loop(0, nc, fill, 0)

    lam = lam_ref[...]
    coef = -LRU_C * (jnp.maximum(-lam, 0.0) + jnp.log(1.0 + jnp.exp(-jnp.abs(lam))))
    cw = cw_ref[...]
    cb = cb_ref[...]
    bg = bg_ref[...]

    def conv_chunk(r0):
        w = xpad_ref[pl.ds(r0 + HALO - SUBLANES, tc + 2 * SUBLANES), :]
        y = (cw[0:1] * _roll_rows(w, 1) + cw[1:2] * w + cw[2:3] * _roll_rows(w, -1)
             + cw[3:4] * _roll_rows(w, -2))
        return y[SUBLANES:SUBLANES + tc] + cb

    def coeffs(xc, d):
        g = jnp.dot(xc.astype(BF16), wg_ref[:, 2 * C * d:2 * C * (d + 1)],
                    preferred_element_type=F32) + bg[:, 2 * C * d:2 * C * (d + 1)]
        r = jax.nn.sigmoid(g[:, :C])
        i = jax.nn.sigmoid(g[:, C:])
        log_a = coef[d:d + 1] * r
        a = jnp.exp(log_a)
        b = jnp.exp2(0.5 * jnp.log2(1.0 - a * a)) * (i * xc)
        return a, b

    lane = lax.broadcasted_iota(jnp.int32, (1, C), 1)
    gdim = C // len(POOL_WINDOWS)

    inv_win = None
    for g, win in enumerate(POOL_WINDOWS):
        inv_g = jnp.full((1, C), 1.0 / win, F32)
        inv_win = inv_g if inv_win is None else jnp.where(lane >= g * gdim, inv_g, inv_win)

    def pool_rows(r0, n, edge):
        w = upad_ref[pl.ds(r0, n + 2 * HALO), :]
        s2 = w + _roll_rows(w, 1)
        s4 = _roll_rows(s2, 1) + _roll_rows(s2, -1)
        s8 = _roll_rows(s4, 2) + _roll_rows(s4, -2)
        s16 = _roll_rows(s8, 4) + _roll_rows(s8, -4)
        sel = s2[HALO:HALO + n]
        for g, s in enumerate((s4, s8, s16), start=1):
            sel = jnp.where(lane >= g * gdim, s[HALO:HALO + n], sel)
        if edge:
            t = r0 + lax.broadcasted_iota(jnp.int32, (n, 1), 0)
            cnt = None
            for g, win in enumerate(POOL_WINDOWS):
                cg = (jnp.minimum(t - win // 2 + win, T) - jnp.maximum(t - win // 2, 0)).astype(F32)
                cg = jnp.broadcast_to(cg, (n, C))
                cnt = cg if cnt is None else jnp.where(lane >= g * gdim, cg, cnt)
            mean = sel / cnt
        else:
            mean = sel * inv_win
        mixed = mean - w[HALO:HALO + n]
        y = jnp.dot(mixed.astype(BF16), wp_ref[...], preferred_element_type=F32) * ps_ref[...]
        p_out_ref[0, pl.ds(r0, n), :] = y.astype(p_out_ref.dtype)

    per = 2 if nc % 2 == 0 else 1
    n_slab = C // LANES

    def fwd(i, h):
        for par in range(per):
            r0 = pl.multiple_of((per * i + par) * tc, tc)
            xc = conv_chunk(r0)
            xc_ref[pl.ds(r0, tc), :] = xc
            a, b = coeffs(xc, 0)
            h = _chunk_scan(a, b, h, sa_ref.at[par], sb_ref.at[par], hf_ref, r0, False)
        return h

    h = lax.fori_loop(0, nc // per, fwd, jnp.broadcast_to(h0_ref[0, 0:1, :], (SUBLANES, C)))
    hfin_ref[0, 0:1, :] = h[0:1]

    def bwd(i, h):
        for par in range(per):
            r0 = pl.multiple_of((nc - 1 - (per * i + par)) * tc, tc)
            rows = pl.ds(r0, tc)
            pool_rows(r0, tc, False)
            a, b = coeffs(xc_ref[rows, :], 1)
            h = _chunk_scan(a, b, h, sa_ref.at[par], sb_ref.at[par], hb_ref.at[par], 0, True)
            y = (jnp.concatenate([hb_ref[par, s] for s in range(n_slab)], axis=1)
                 + jnp.concatenate([hf_ref[s, rows, :] for s in range(n_slab)], axis=1))
            a_out_ref[0, rows, :] = (y * jax.nn.gelu(lg_ref[0, rows, :])).astype(a_out_ref.dtype)
        return h

    h = lax.fori_loop(0, nc // per, bwd, jnp.broadcast_to(h0_ref[0, 1:2, :], (SUBLANES, C)))
    hfin_ref[0, 1:2, :] = h[0:1]

    pool_rows(0, BF16_ROWS, True)
    pool_rows(T - BF16_ROWS, BF16_ROWS, True)


def _seq_call(lx, lg, pu, h0, lw, tc):
    B, T, C = lx.shape
    seq = pl.BlockSpec((1, T, C), lambda b: (b, 0, 0))
    st = pl.BlockSpec((1, 2, C), lambda b: (b, 0, 0))
    return pl.pallas_call(
        functools.partial(_seq_kernel, tc=tc),
        grid=(B,),
        in_specs=[seq, seq, seq, st, _const_spec((CONV_W, C)), _const_spec((1, C)),
                  _const_spec((C, 4 * C)), _const_spec((1, 4 * C)), _const_spec((2, C)),
                  _const_spec((C, C)), _const_spec((1, C))],
        out_specs=[seq, seq, st],
        out_shape=[jax.ShapeDtypeStruct((B, T, C), BF16), jax.ShapeDtypeStruct((B, T, C), BF16),
                   jax.ShapeDtypeStruct((B, 2, C), F32)],
        scratch_shapes=[pltpu.VMEM((T + 2 * HALO, C), F32), pltpu.VMEM((T + 2 * HALO, C), F32),
                        pltpu.VMEM((C // LANES, T, LANES), F32), pltpu.VMEM((T, C), F32),
                        pltpu.VMEM((2, C // LANES, tc, LANES), F32),
                        pltpu.VMEM((2, C // LANES, tc, LANES), F32),
                        pltpu.VMEM((2, C // LANES, tc, LANES), F32)],
        compiler_params=_cparams(1),
        name="seq_mix",
    )(lx, lg, pu, h0, lw["conv_w"], lw["conv_b"], lw["w_gate"], lw["b_gate"], lw["lam"],
      lw["w_pool"], lw["pool_scale"])


def _attn_kernel(*refs, n_seg, tq_sub, kc):
    q_ref = refs[0]
    kv_refs = refs[1:1 + 2 * n_seg]
    o_ref = refs[1 + 2 * n_seg]
    s_scr = refs[2 + 2 * n_seg]
    n_sub = q_ref.shape[2] // tq_sub
    units = [(hh, j) for j in range(n_sub) for hh in range(2)]
    chunks = []
    off = 0
    for seg in range(n_seg):
        lk = kv_refs[2 * seg].shape[2]
        for st in range(0, lk, kc):
            sz = min(kc, lk - st)
            chunks.append((seg, st, sz, off))
            off += sz
    nch = len(chunks)
    mvec = {}
    col_max = {}
    acc = {}
    results = {}
    pv_hist = []

    def score_chunk(u, c):
        hh, j = units[u]
        seg, st, sz, off = chunks[c]
        q = q_ref[0, hh, j * tq_sub:(j + 1) * tq_sub, :]
        k = kv_refs[2 * seg][0, hh, st:st + sz, :]
        s = lax.dot_general(k, q, (((1,), (1,)), ((), ())), preferred_element_type=F32)
        s_scr[u % 2, off:off + sz, :] = s
        cm = jnp.max(s.reshape(sz // SUBLANES, SUBLANES, tq_sub), axis=0)
        mvec[u] = cm if c == 0 else jnp.maximum(mvec[u], cm)
        if c == nch - 1:
            col_max[u] = jnp.max(mvec[u], axis=0, keepdims=True)

    def weight_chunk(u, c):
        hh, j = units[u]
        seg, st, sz, off = chunks[c]
        m = col_max[u]
        if len(pv_hist) >= ORDER_DEPTH:
            bits = pltpu.bitcast(pv_hist[-ORDER_DEPTH][MLA_V:MLA_V + 1], jnp.uint32)
            m = m + ((bits >> 16) >> 16).astype(F32)
        p = jnp.exp2((s_scr[u % 2, off:off + sz, :] - m).astype(BF16))
        vt = kv_refs[2 * seg + 1][0, 0, hh * V_ROWS:(hh + 1) * V_ROWS, st:st + sz]
        pv = jnp.dot(vt, p, preferred_element_type=F32)
        pv_hist.append(pv)
        acc[u] = pv if c == 0 else acc[u] + pv
        if c == nch - 1:
            results[units[u]] = acc[u][:MLA_V] / acc[u][MLA_V:MLA_V + 1]

    lead = nch + min(SCORE_LEAD, nch - 1)
    events = []
    for u in range(len(units)):
        for c in range(nch):
            events.append((u * nch + c - lead, 0, u, c))
            events.append((u * nch + c, 1, u, c))
    for _, kind, u, c in sorted(events):
        (weight_chunk if kind else score_chunk)(u, c)

    for j in range(n_sub):
        o_t = jnp.concatenate([results[(0, j)], results[(1, j)]], axis=0)
        o_ref[0, j * tq_sub:(j + 1) * tq_sub, :] = o_t.T.astype(o_ref.dtype)


def _attn_call(q, segs, tq):
    B, H, T, _ = q.shape
    tq_sub = min(tq, ATTN_Q_UNIT)
    in_specs = [pl.BlockSpec((1, 2, tq, LANES), lambda b, p, i: (b, p, i, 0))]
    args = [q]
    lk_total = 0
    for k, v in segs:
        lk = k.shape[2]
        lk_total += lk
        in_specs.append(pl.BlockSpec((1, 2, lk, LANES), lambda b, p, i: (b, p, 0, 0)))
        in_specs.append(pl.BlockSpec((1, 1, 2 * V_ROWS, lk), lambda b, p, i: (b, p, 0, 0)))
        args += [k, v]
    return pl.pallas_call(
        functools.partial(_attn_kernel, n_seg=len(segs), tq_sub=tq_sub, kc=ATTN_KEY_CHUNK),
        grid=(B, H // 2, T // tq),
        in_specs=in_specs,
        out_specs=pl.BlockSpec((1, tq, LANES), lambda b, p, i: (b, i, p)),
        out_shape=jax.ShapeDtypeStruct((B, T, H * MLA_V), BF16),
        scratch_shapes=[pltpu.VMEM((2, lk_total, tq_sub), F32)],
        compiler_params=_cparams(3),
        name="attention",
    )(*args)


def _post_kernel(x_ref, a_ref, att_ref, p_ref, g1_ref, sh_ref, sc_ref, g2_ref, gn_ref, wo_ref,
                 w1_ref, w2_ref, o_ref, *, ff_chunk):
    d_lru = a_ref.shape[2]
    d_att = att_ref.shape[2]
    y = jnp.dot(a_ref[0], wo_ref[0:d_lru, :], preferred_element_type=F32)
    y += jnp.dot(att_ref[0], wo_ref[d_lru:d_lru + d_att, :], preferred_element_type=F32)
    y += jnp.dot(p_ref[0], wo_ref[d_lru + d_att:, :], preferred_element_type=F32)
    x1 = x_ref[0] + g1_ref[0] * y
    xn = x1 * lax.rsqrt(jnp.mean(x1 * x1, axis=-1, keepdims=True) + EPS)
    h = ((xn * gn_ref[...]) * (1.0 + sc_ref[0]) + sh_ref[0]).astype(BF16)
    d_ff = w1_ref.shape[1]
    acc = None
    for c in range(d_ff // ff_chunk):
        u = jnp.dot(h, w1_ref[:, c * ff_chunk:(c + 1) * ff_chunk], preferred_element_type=F32)
        u = jnp.square(jnp.maximum(u, 0.0)).astype(BF16)
        part = jnp.dot(u, w2_ref[c * ff_chunk:(c + 1) * ff_chunk, :], preferred_element_type=F32)
        acc = part if acc is None else acc + part
    o_ref[0] = x1 + g2_ref[0] * acc


def _post_call(x, a, att, p, g1, sh, sc, g2, gn, lw, tm):
    B, T, D = x.shape
    tok = lambda w: pl.BlockSpec((1, tm, w), lambda b, i: (b, i, 0))
    vec = pl.BlockSpec((1, 1, D), lambda b, i: (b, 0, 0))
    single = lambda shape: pl.BlockSpec(shape, lambda b, i: (0,) * len(shape),
                                        pipeline_mode=pl.Buffered(1))
    return pl.pallas_call(
        functools.partial(_post_kernel, ff_chunk=FF_CHUNK),
        grid=(B, T // tm),
        in_specs=[tok(D), tok(a.shape[2]), tok(att.shape[2]), tok(p.shape[2]), vec, vec, vec, vec,
                  _const_spec((1, D)), single(lw["w_out"].shape), single(lw["w_ff1"].shape),
                  single(lw["w_ff2"].shape)],
        out_specs=tok(D),
        out_shape=jax.ShapeDtypeStruct((B, T, D), F32),
        compiler_params=_cparams(2),
        name="out_proj_mlp",
    )(x, a, att, p, g1, sh, sc, g2, gn, lw["w_out"], lw["w_ff1"], lw["w_ff2"])


def _rope_partner():
    j = np.arange(MLA_ROPE)
    half = (j % (2 * ROPE_FREQS)) // ROPE_FREQS
    return np.where(half == 0, j + ROPE_FREQS, j - ROPE_FREQS)


def _block_diag(w):
    n, a, b = w.shape
    eye = jnp.asarray(np.eye(n, dtype=np.float32), w.dtype)
    return (w[:, :, None, :] * eye[:, None, :, None]).reshape(n * a, n * b)


def _layer_weights(l, w_in, conv_w, conv_b, lru_w_a, lru_b_a, lru_w_x, lru_b_x, lru_lambda, g_q_lat,
                   w_uq, g_kv_lat, w_ukv, g_qn, g_kn, w_pool, pool_scale, w_out, w_ff1, w_ff2):
    d = w_in.shape[1]
    d_lru = conv_w.shape[2]
    q_rank = w_uq.shape[1]
    kv_rank = w_ukv.shape[1]
    heads = w_uq.shape[2] // MLA_QK
    partner = _rope_partner()
    o_kr = 2 * d_lru + q_rank + kv_rank
    wi = w_in[l]
    kr = wi[:, o_kr:o_kr + MLA_ROPE]
    krb = jnp.concatenate([jnp.zeros((d, MLA_NOPE), F32), kr, kr[:, partner]], axis=1)
    w_in_p = jnp.concatenate([wi[:, :o_kr], krb, wi[:, o_kr + MLA_ROPE:]], axis=1).astype(BF16)
    wq = w_uq[l].reshape(q_rank, heads, MLA_QK)
    pad = jnp.zeros((q_rank, heads, HEAD_PAD - MLA_QK), F32)
    wq_main = jnp.concatenate([wq, pad], axis=2)
    wq_partner = wq[:, :, MLA_NOPE + partner]
    wq = jnp.concatenate([wq_main.reshape(q_rank, heads * HEAD_PAD),
                          wq_partner.reshape(q_rank, heads * MLA_ROPE)], axis=1)
    wkv = w_ukv[l].reshape(kv_rank, heads, MLA_NOPE + MLA_V)
    wk = jnp.concatenate([wkv[:, :, :MLA_NOPE], jnp.zeros((kv_rank, heads, HEAD_PAD - MLA_NOPE), F32)],
                         axis=2).reshape(kv_rank, heads * HEAD_PAD)
    wv = wkv[:, :, MLA_NOPE:].reshape(kv_rank, heads * MLA_V)
    zpad = jnp.zeros((HEAD_PAD - MLA_QK,), F32)
    znope = jnp.zeros((MLA_NOPE,), F32)
    q_scale = LOG2E / math.sqrt(MLA_QK)
    gq = jnp.concatenate([g_qn[l], zpad]) * q_scale
    gq_sw = jnp.concatenate([znope, g_qn[l][MLA_NOPE + partner], zpad]) * q_scale
    gk = jnp.concatenate([g_kn[l], zpad])
    gk_sw = jnp.concatenate([znope, g_kn[l][MLA_NOPE + partner], zpad])
    head_ones = jnp.asarray(np.kron(np.eye(2), np.ones((HEAD_PAD, HEAD_PAD))), BF16)
    w_gate = jnp.concatenate([_block_diag(lru_w_a[l, 0]), _block_diag(lru_w_x[l, 0]),
                              _block_diag(lru_w_a[l, 1]), _block_diag(lru_w_x[l, 1])], axis=1)
    b_gate = jnp.concatenate([lru_b_a[l, 0], lru_b_x[l, 0], lru_b_a[l, 1], lru_b_x[l, 1]])
    return dict(
        heads=heads, d_lru=d_lru, q_rank=q_rank, kv_rank=kv_rank,
        w_in=w_in_p, g_q_lat=g_q_lat[l][None], w_uq=wq.astype(BF16), g_kv_lat=g_kv_lat[l][None],
        w_k=wk.astype(BF16), w_v=wv.astype(BF16), head_gains=jnp.stack([gq, gq_sw, gk, gk_sw]),
        head_ones=head_ones,
        conv_w=conv_w[l], conv_b=conv_b[l][None], w_gate=w_gate.astype(BF16), b_gate=b_gate[None],
        lam=lru_lambda[l], w_pool=_block_diag(w_pool[l]).astype(BF16), pool_scale=pool_scale[l][None],
        w_out=w_out[l].astype(BF16), w_ff1=w_ff1[l].astype(BF16), w_ff2=w_ff2[l].astype(BF16))


def _rope_tables(T):
    f32 = np.float32
    j = np.arange(MLA_ROPE)
    axis = j // (2 * ROPE_FREQS)
    half = (j % (2 * ROPE_FREQS)) // ROPE_FREQS
    f = j % ROPE_FREQS
    t = np.arange(T)
    pos = np.stack([(t // GRID_W).astype(f32), (t % GRID_W).astype(f32)], axis=-1)
    freqs = np.power(f32(ROPE_BASE), -np.arange(ROPE_FREQS, dtype=f32) / f32(ROPE_FREQS)).astype(f32)
    ang = (pos[:, axis] * freqs[f]).astype(f32)
    sign = np.where(half == 0, -1.0, 1.0).astype(f32)
    ct = np.concatenate([np.ones((T, MLA_NOPE), f32), np.cos(ang).astype(f32),
                         np.zeros((T, HEAD_PAD - MLA_QK), f32)], axis=1)
    st = np.concatenate([np.zeros((T, MLA_NOPE), f32), (np.sin(ang) * sign).astype(f32),
                         np.zeros((T, HEAD_PAD - MLA_QK), f32)], axis=1)
    return jnp.asarray(ct), jnp.asarray(st)


def _no_rope_tables(T):
    lane = np.arange(HEAD_PAD)
    ct = np.broadcast_to((lane < MLA_QK).astype(np.float32), (T, HEAD_PAD))
    return jnp.asarray(ct), jnp.zeros((T, HEAD_PAD), F32)


def _tile(n, pref):
    t = min(n, pref)
    while n % t:
        t //= 2
    return t


def kernel(x, c, ctx, c_ctx, w_mod, b_mod, g_norm1, g_norm2, w_in, conv_w, conv_b, lru_w_a, lru_b_a,
           lru_w_x, lru_b_x, lru_lambda, g_q_lat, w_uq, g_kv_lat, w_ukv, g_qn, g_kn, w_pool, pool_scale,
           w_out, w_ff1, w_ff2):
    B, L, D = x.shape
    Lc = ctx.shape[1]
    depth = w_mod.shape[0]
    d_lru = conv_w.shape[2]

    rows = -(-(B + 1) // SUBLANES) * SUBLANES
    cvec = jnp.concatenate([c, c_ctx[None], jnp.zeros((rows - B - 1, D), F32)], axis=0)
    mods = _mod_call(cvec, w_mod, b_mod)

    rope_l = _rope_tables(L)
    rope_c = _no_rope_tables(Lc)
    tm_l, tm_c = _tile(L, TOKEN_STEP), _tile(Lc, TOKEN_STEP)
    tp_l, tp_c = _tile(L, TOKEN_STEP), _tile(Lc, TOKEN_STEP)
    tq_l, tq_c = _tile(L, ATTN_Q_STEP), _tile(Lc, ATTN_Q_STEP)
    tc_l, tc_c = _tile(L, ROW_BLOCK), _tile(Lc, ROW_BLOCK)

    h = ctx
    zeros_state = jnp.zeros((B, 2, d_lru), F32)
    for l in range(depth):
        last = l == depth - 1
        lw = _layer_weights(l, w_in, conv_w, conv_b, lru_w_a, lru_b_a, lru_w_x, lru_b_x, lru_lambda,
                            g_q_lat, w_uq, g_kv_lat, w_ukv, g_qn, g_kn, w_pool, pool_scale, w_out,
                            w_ff1, w_ff2)
        ml = mods[l, :B].reshape(B, 1, 6, D)
        mc = jnp.broadcast_to(mods[l, B].reshape(1, 1, 6, D), (B, 1, 6, D))
        sh1, sc1, g1, sh2, sc2, g2 = (ml[:, :, i] for i in range(6))
        csh1, csc1, cg1, csh2, csc2, cg2 = (mc[:, :, i] for i in range(6))
        gn1 = g_norm1[l][None]
        gn2 = g_norm2[l][None]

        lx_c, lg_c, pu_c, q_c, k_c, v_c = _inproj_call(h, csh1, csc1, gn1, lw, rope_c, tm_c)
        a_c, p_c, hfin = _seq_call(lx_c, lg_c, pu_c, zeros_state, lw, tc_c)
        lx_l, lg_l, pu_l, q_l, k_l, v_l = _inproj_call(x, sh1, sc1, gn1, lw, rope_l, tm_l)
        a_l, p_l, _ = _seq_call(lx_l, lg_l, pu_l, hfin, lw, tc_l)
        att_l = _attn_call(q_l, [(k_c, v_c), (k_l, v_l)], tq_l)
        x = _post_call(x, a_l, att_l, p_l, g1, sh2, sc2, g2, gn2, lw, tp_l)
        if not last:
            att_c = _attn_call(q_c, [(k_c, v_c)], tq_c)
            h = _post_call(h, a_c, att_c, p_c, cg1, csh2, csc2, cg2, gn2, lw, tp_c)
    return x
```
